```python
import math
import jax, jax.numpy as jnp
from jax import lax
import numpy as np

D_MODEL = 1024
BATCH = 16
SEQ = 4096
DEPTH = 1
DEC_BATCH = 4
DEC_SEQ = 8192
PAST_LEN = 128

N_MEM = 256
MLA_HEADS = 8
NOPE_DIM = 64
ROPE_DIM = 32
V_DIM = 64
QK_DIM = NOPE_DIM + ROPE_DIM
Q_LORA = 384
KV_LORA = 256
ATT_W = MLA_HEADS * V_DIM
Q_BLOCK = 128
ROPE_THETA = 10000.0
SSM_W = D_MODEL - ATT_W
SSM_GC = 16
SSM_G = SSM_W // SSM_GC
SSM_P = 64
MIX_W = ATT_W + SSM_W
IN_W = Q_LORA + KV_LORA + ROPE_DIM + SSM_W
XA_HEADS = 4
XA_DIM = D_MODEL // XA_HEADS
N_GROUPS = 4
EXP_PER_GROUP = 8
N_EXPERTS = N_GROUPS * EXP_PER_GROUP
TOP_K = 2
D_FF_EXPERT = 256
ALPHA = (2 * DEPTH) ** 0.25
BETA = (8 * DEPTH) ** -0.25
LN_EPS = 1e-5
RMS_EPS = 1e-6

kernel_name = 'hybrid_mla_s5_hmoe_encoder'

F32 = jnp.float32


def layer_norm(x, g, b):
    xf = x.astype(F32)
    mu = jnp.mean(xf, -1, keepdims=True)
    var = jnp.mean(jnp.square(xf - mu), -1, keepdims=True)
    return ((xf - mu) * lax.rsqrt(var + LN_EPS) * g.astype(F32) + b.astype(F32)).astype(x.dtype)


def rms_norm(x, g):
    xf = x.astype(F32)
    return (xf * lax.rsqrt(jnp.mean(xf * xf, -1, keepdims=True) + RMS_EPS) * g.astype(F32)).astype(x.dtype)


def rope_tables(seq, dtype):
    inv = ROPE_THETA ** (-jnp.arange(0, ROPE_DIM, 2, dtype=F32) / ROPE_DIM)
    ang = jnp.arange(seq, dtype=F32)[:, None] * inv[None, :]
    return jnp.cos(ang).astype(dtype), jnp.sin(ang).astype(dtype)


def apply_rope(x, cos, sin):
    shape = (x.shape[1],) + (1,) * (x.ndim - 3) + (ROPE_DIM // 2,)
    c = cos.reshape(shape)
    s = sin.reshape(shape)
    x1, x2 = jnp.split(x, 2, axis=-1)
    return jnp.concatenate([x1 * c - x2 * s, x1 * s + x2 * c], axis=-1)


def mla_attention(q_nope, q_rope, k_nope, k_rope, v):
    b, s, h, _ = q_nope.shape
    nb = s // Q_BLOCK
    scale = QK_DIM ** -0.5

    def blocks(t):
        return jnp.moveaxis(t.reshape((b, nb, Q_BLOCK) + t.shape[2:]), 1, 0)

    def one_block(qs):
        qn, qr = qs
        sc = (jnp.einsum('bqhd,bkhd->bhqk', qn, k_nope)
              + jnp.einsum('bqhr,bkr->bhqk', qr, k_rope))
        p = jax.nn.softmax(sc.astype(F32) * scale, axis=-1)
        return jnp.einsum('bhqk,bkhd->bqhd', p.astype(v.dtype), v)

    o = lax.map(one_block, (blocks(q_nope), blocks(q_rope)))
    return jnp.moveaxis(o, 0, 1).reshape(b, s, h * V_DIM)


def s5_bidirectional(u, a_re, a_im, log_dt, b_re, b_im, c_re, c_im, d_skip):
    bsz, s, _ = u.shape
    uf = u.astype(F32).reshape(bsz, s, SSM_G, SSM_GC)
    A = lax.complex(a_re.astype(F32), a_im.astype(F32))
    dt = jnp.exp(log_dt.astype(F32))[..., None]
    a_bar = jnp.exp(A * dt)
    Bm = lax.complex(b_re.astype(F32), b_im.astype(F32))
    b_bar = ((a_bar - 1.0) / A)[..., None] * Bm
    Cm = lax.complex(c_re.astype(F32), c_im.astype(F32))

    def combine(e1, e2):
        a1, x1 = e1
        a2, x2 = e2
        return a1 * a2, a2 * x1 + x2

    def one_seq(us):
        uc = us.astype(jnp.complex64)
        y = jnp.zeros(us.shape, F32)
        for direction in range(2):
            bu = jnp.einsum('gpc,sgc->sgp', b_bar[direction], uc)
            a = jnp.broadcast_to(a_bar[direction], bu.shape)
            _, hs = lax.associative_scan(combine, (a, bu), axis=0, reverse=(direction == 1))
            y = y + jnp.real(jnp.einsum('gcp,sgp->sgc', Cm[direction], hs))
        return y

    y = lax.map(one_seq, uf).reshape(bsz, s, SSM_W)
    y = y + d_skip.astype(F32) * u.astype(F32)
    return y.astype(u.dtype)


def hier_moe(x, w_rg, b_rg, w_re, b_re, w_e1, w_e3, w_e2):
    b, s, d = x.shape
    t = x.reshape(b * s, d)
    lg = (t @ w_rg).astype(F32) + b_rg.astype(F32)
    grp = jnp.argmax(lg, axis=-1)
    p_grp = jnp.take_along_axis(jax.nn.softmax(lg, axis=-1), grp[:, None], axis=-1)
    le = ((t @ w_re).astype(F32) + b_re.astype(F32)).reshape(-1, N_GROUPS, EXP_PER_GROUP)
    le_sel = jnp.take_along_axis(le, grp[:, None, None], axis=1)[:, 0]
    top_v, top_i = lax.top_k(le_sel, TOP_K)
    w_k = jax.nn.softmax(top_v, axis=-1) * p_grp
    within = jnp.einsum('tk,tke->te', w_k, jax.nn.one_hot(top_i, EXP_PER_GROUP, dtype=F32))
    gates = (jax.nn.one_hot(grp, N_GROUPS, dtype=F32)[:, :, None]
             * within[:, None, :]).reshape(-1, N_EXPERTS)

    def expert_step(acc, xs):
        w1, w3, w2, gt = xs
        hdn = jax.nn.silu(t @ w1) * (t @ w3)
        return acc + gt[:, None] * (hdn @ w2).astype(F32), None

    acc0 = jnp.zeros((t.shape[0], d), F32)
    out, _ = lax.scan(expert_step, acc0, (w_e1, w_e3, w_e2, gates.T))
    return out.astype(x.dtype).reshape(b, s, d)


def encoder_layer(x, mem, cos, sin,
                  w_in, g_q_lat, w_uq, g_kv_lat, w_ukv,
                  ssm_a_re, ssm_a_im, ssm_log_dt, ssm_b_re, ssm_b_im, ssm_c_re, ssm_c_im, ssm_d, w_glu,
                  g_attn_grp, g_ssm_grp, w_out, ln1_g, ln1_b,
                  w_cq, w_ckv, w_co, ln2_g, ln2_b,
                  w_rg, b_rg, w_re, b_re, w_e1, w_e3, w_e2, ln3_g, ln3_b):
    b, s, _ = x.shape
    proj = x @ w_in
    q_lat, kv_lat, k_pe, u = jnp.split(
        proj, [Q_LORA, Q_LORA + KV_LORA, Q_LORA + KV_LORA + ROPE_DIM], axis=-1)
    q = (rms_norm(q_lat, g_q_lat) @ w_uq).reshape(b, s, MLA_HEADS, QK_DIM)
    q_nope = q[..., :NOPE_DIM]
    q_rope = apply_rope(q[..., NOPE_DIM:], cos, sin)
    kv = (rms_norm(kv_lat, g_kv_lat) @ w_ukv).reshape(b, s, MLA_HEADS, NOPE_DIM + V_DIM)
    k_nope = kv[..., :NOPE_DIM]
    v = kv[..., NOPE_DIM:]
    k_rope = apply_rope(k_pe, cos, sin)
    att = mla_attention(q_nope, q_rope, k_nope, k_rope, v)
    y = s5_bidirectional(u, ssm_a_re, ssm_a_im, ssm_log_dt, ssm_b_re, ssm_b_im,
                         ssm_c_re, ssm_c_im, ssm_d)
    gy = jax.nn.gelu(y)
    ssm = gy * jax.nn.sigmoid(gy @ w_glu)
    mixed = jnp.concatenate([rms_norm(att, g_attn_grp), rms_norm(ssm, g_ssm_grp)], axis=-1) @ w_out
    x = layer_norm(ALPHA * x + mixed, ln1_g, ln1_b)
    m = mem.shape[1]
    qc = (x @ w_cq).reshape(b, s, XA_HEADS, XA_DIM)
    kvc = (mem @ w_ckv).reshape(b, m, 2, XA_HEADS, XA_DIM)
    kc = kvc[:, :, 0]
    vc = kvc[:, :, 1]
    sc = jnp.einsum('bqhd,bkhd->bhqk', qc, kc).astype(F32) * (XA_DIM ** -0.5)
    pc = jax.nn.softmax(sc, axis=-1)
    ca = jnp.einsum('bhqk,bkhd->bqhd', pc.astype(x.dtype), vc).reshape(b, s, D_MODEL) @ w_co
    x = layer_norm(ALPHA * x + ca, ln2_g, ln2_b)
    moe = hier_moe(x, w_rg, b_rg, w_re, b_re, w_e1, w_e3, w_e2)
    x = layer_norm(ALPHA * x + moe, ln3_g, ln3_b)
    return x


def setup_inputs(seed: int = 0) -> dict:
    key = jax.random.key(seed)
    ks = iter(jax.random.split(key, 48))
    L = DEPTH

    def nrm(shape, scale):
        return jax.random.normal(next(ks), shape, F32) * scale

    def gain(n):
        return 1.0 + nrm((L, n), 0.02)

    x_prompt = nrm((BATCH, SEQ, D_MODEL), 1.0)
    x_sample = nrm((DEC_BATCH, DEC_SEQ, D_MODEL), 1.0)
    mem_prompt = nrm((BATCH, N_MEM, D_MODEL), 1.0)
    mem_sample = nrm((DEC_BATCH, N_MEM, D_MODEL), 1.0)

    w_in = nrm((L, D_MODEL, IN_W), D_MODEL ** -0.5)
    g_q_lat = gain(Q_LORA)
    w_uq = nrm((L, Q_LORA, MLA_HEADS * QK_DIM), Q_LORA ** -0.5)
    g_kv_lat = gain(KV_LORA)
    w_ukv = nrm((L, KV_LORA, MLA_HEADS * (NOPE_DIM + V_DIM)), KV_LORA ** -0.5)

    ssm_shape = (L, 2, SSM_G, SSM_P)
    ssm_a_re = -0.5 + nrm(ssm_shape, 0.01)
    ssm_a_im = jnp.broadcast_to(jnp.pi * jnp.arange(SSM_P, dtype=F32), ssm_shape) + nrm(ssm_shape, 0.01)
    ssm_log_dt = jax.random.uniform(next(ks), (L, 2, SSM_G), F32,
                                    minval=math.log(1e-3), maxval=math.log(1e-1))
    ssm_b_re = nrm((L, 2, SSM_G, SSM_P, SSM_GC), (2 * SSM_GC) ** -0.5)
    ssm_b_im = nrm((L, 2, SSM_G, SSM_P, SSM_GC), (2 * SSM_GC) ** -0.5)
    ssm_c_re = nrm((L, 2, SSM_G, SSM_GC, SSM_P), (2 * SSM_P) ** -0.5)
    ssm_c_im = nrm((L, 2, SSM_G, SSM_GC, SSM_P), (2 * SSM_P) ** -0.5)
    ssm_d = nrm((L, SSM_W), 1.0)
    w_glu = nrm((L, SSM_W, SSM_W), SSM_W ** -0.5)

    g_attn_grp = gain(ATT_W)
    g_ssm_grp = gain(SSM_W)
    w_out = nrm((L, MIX_W, D_MODEL), BETA * MIX_W ** -0.5)
    ln1_g = gain(D_MODEL)
    ln1_b = nrm((L, D_MODEL), 0.02)

    w_cq = nrm((L, D_MODEL, D_MODEL), D_MODEL ** -0.5)
    w_ckv = nrm((L, D_MODEL, 2 * D_MODEL), D_MODEL ** -0.5)
    w_co = nrm((L, D_MODEL, D_MODEL), BETA * D_MODEL ** -0.5)
    ln2_g = gain(D_MODEL)
    ln2_b = nrm((L, D_MODEL), 0.02)

    w_rg = nrm((L, D_MODEL, N_GROUPS), D_MODEL ** -0.5)
    b_rg = nrm((L, N_GROUPS), 0.01)
    w_re = nrm((L, D_MODEL, N_EXPERTS), D_MODEL ** -0.5)
    b_re = nrm((L, N_EXPERTS), 0.01)
    w_e1 = nrm((L, N_EXPERTS, D_MODEL, D_FF_EXPERT), D_MODEL ** -0.5)
    w_e3 = nrm((L, N_EXPERTS, D_MODEL, D_FF_EXPERT), D_MODEL ** -0.5)
    w_e2 = nrm((L, N_EXPERTS, D_FF_EXPERT, D_MODEL), BETA * D_FF_EXPERT ** -0.5)
    ln3_g = gain(D_MODEL)
    ln3_b = nrm((L, D_MODEL), 0.02)

    return {'x_prompt': x_prompt, 'x_sample': x_sample, 'mem_prompt': mem_prompt, 'mem_sample': mem_sample,
            'w_in': w_in, 'g_q_lat': g_q_lat, 'w_uq': w_uq, 'g_kv_lat': g_kv_lat, 'w_ukv': w_ukv,
            'ssm_a_re': ssm_a_re, 'ssm_a_im': ssm_a_im, 'ssm_log_dt': ssm_log_dt,
            'ssm_b_re': ssm_b_re, 'ssm_b_im': ssm_b_im, 'ssm_c_re': ssm_c_re, 'ssm_c_im': ssm_c_im,
            'ssm_d': ssm_d, 'w_glu': w_glu,
            'g_attn_grp': g_attn_grp, 'g_ssm_grp': g_ssm_grp, 'w_out': w_out, 'ln1_g': ln1_g, 'ln1_b': ln1_b,
            'w_cq': w_cq, 'w_ckv': w_ckv, 'w_co': w_co, 'ln2_g': ln2_g, 'ln2_b': ln2_b,
            'w_rg': w_rg, 'b_rg': b_rg, 'w_re': w_re, 'b_re': b_re,
            'w_e1': w_e1, 'w_e3': w_e3, 'w_e2': w_e2, 'ln3_g': ln3_g, 'ln3_b': ln3_b}


def reference(x_prompt, x_sample, mem_prompt, mem_sample,
              w_in, g_q_lat, w_uq, g_kv_lat, w_ukv,
              ssm_a_re, ssm_a_im, ssm_log_dt, ssm_b_re, ssm_b_im, ssm_c_re, ssm_c_im, ssm_d, w_glu,
              g_attn_grp, g_ssm_grp, w_out, ln1_g, ln1_b,
              w_cq, w_ckv, w_co, ln2_g, ln2_b,
              w_rg, b_rg, w_re, b_re, w_e1, w_e3, w_e2, ln3_g, ln3_b):
    def run(x, mem):
        cos, sin = rope_tables(x.shape[1], x.dtype)
        for l in range(DEPTH):
            x = encoder_layer(x, mem, cos, sin,
                              w_in[l], g_q_lat[l], w_uq[l], g_kv_lat[l], w_ukv[l],
                              ssm_a_re[l], ssm_a_im[l], ssm_log_dt[l], ssm_b_re[l], ssm_b_im[l],
                              ssm_c_re[l], ssm_c_im[l], ssm_d[l], w_glu[l],
                              g_attn_grp[l], g_ssm_grp[l], w_out[l], ln1_g[l], ln1_b[l],
                              w_cq[l], w_ckv[l], w_co[l], ln2_g[l], ln2_b[l],
                              w_rg[l], b_rg[l], w_re[l], b_re[l], w_e1[l], w_e3[l], w_e2[l],
                              ln3_g[l], ln3_b[l])
        return x

    y_prompt = run(x_prompt, mem_prompt)
    y_sample = run(x_sample, mem_sample)
    return (y_prompt, y_sample)
```

```python
import functools
import math

import jax
import jax.numpy as jnp
from jax import lax
from jax.experimental import pallas as pl
from jax.experimental.pallas import tpu as pltpu

F32 = jnp.float32
BF16 = jnp.bfloat16

D_MODEL = 1024
MLA_HEADS = 8
NOPE_DIM = 64
ROPE_DIM = 32
V_DIM = 64
QK_DIM = NOPE_DIM + ROPE_DIM
Q_LORA = 384
KV_LORA = 256
ATT_W = MLA_HEADS * V_DIM
ROPE_THETA = 10000.0
SSM_W = D_MODEL - ATT_W
SSM_GC = 16
SSM_G = SSM_W // SSM_GC
SSM_P = 64
XA_HEADS = 4
XA_DIM = D_MODEL // XA_HEADS
N_GROUPS = 4
EXP_PER_GROUP = 8
N_EXPERTS = N_GROUPS * EXP_PER_GROUP
TOP_K = 2
D_FF_EXPERT = 256
DEPTH = 1
ALPHA = (2 * DEPTH) ** 0.25
LN_EPS = 1e-5
RMS_EPS = 1e-6

LANE = 128
HEAD_PAD = 128
SSM_L = 16
SSM_CW = SSM_L * SSM_GC
SSM_GB = 4
ROUTE_W = 128
VMEM_LIMIT = 56 * 1024 * 1024


def _cparams(sem):
    return pltpu.CompilerParams(dimension_semantics=sem, vmem_limit_bytes=VMEM_LIMIT)


def _rms(x, g):
    return x * lax.rsqrt(jnp.mean(x * x, axis=-1, keepdims=True) + RMS_EPS) * g


def _ln(x, g, b):
    mu = jnp.mean(x, axis=-1, keepdims=True)
    xc = x - mu
    var = jnp.mean(xc * xc, axis=-1, keepdims=True)
    return xc * lax.rsqrt(var + LN_EPS) * g + b


def _dot(a, b):
    return jnp.dot(a, b, preferred_element_type=F32)


def _dot_t(a, b):
    return lax.dot_general(a, b, (((1,), (1,)), ((), ())), preferred_element_type=F32)


def _matmul_kernel(x_ref, w_ref, o_ref):
    o_ref[...] = _dot(x_ref[...].astype(BF16), w_ref[...]).astype(o_ref.dtype)


def _matmul(x, w, out_dtype, tm, tn):
    m, k = x.shape
    n = w.shape[1]
    return pl.pallas_call(
        _matmul_kernel,
        grid=(m // tm, n // tn),
        in_specs=[pl.BlockSpec((tm, k), lambda i, j: (i, 0)),
                  pl.BlockSpec((k, tn), lambda i, j: (0, j))],
        out_specs=pl.BlockSpec((tm, tn), lambda i, j: (i, j)),
        out_shape=jax.ShapeDtypeStruct((m, n), out_dtype),
        compiler_params=_cparams(("parallel", "parallel")),
        name="mem_kv",
    )(x, w)


def _rot(blk, cos, sin):
    return blk * cos + pltpu.roll(blk, HEAD_PAD - ROPE_DIM, axis=1) * sin


def _in_proj_kernel(x_ref, w_in_ref, gq_ref, gkv_ref, w_uq_ref, w_kn_ref, w_v_ref, vone_ref,
                    cq_ref, sq_ref, ck_ref, sk_ref,
                    q_ref, k_ref, v_ref, u_ref):
    xb = x_ref[...].astype(BF16)
    proj = _dot(xb, w_in_ref[...])
    o_kv = Q_LORA
    o_u = Q_LORA + KV_LORA
    o_kr = o_u + SSM_W
    u_ref[...] = proj[:, o_u:o_kr]
    qn = _rms(proj[:, :Q_LORA], gq_ref[...]).astype(BF16)
    kvn = _rms(proj[:, o_kv:o_u], gkv_ref[...]).astype(BF16)
    qa = _dot(qn, w_uq_ref[...])
    kn = _dot(kvn, w_kn_ref[...])
    v = _dot(kvn, w_v_ref[...]) + vone_ref[...]
    v_ref[...] = v.astype(BF16)
    cq, sq, ck, sk = cq_ref[...], sq_ref[...], ck_ref[...], sk_ref[...]
    kr = _rot(proj[:, o_kr:o_kr + HEAD_PAD], ck, sk)
    for h in range(MLA_HEADS):
        sl = slice(h * HEAD_PAD, (h + 1) * HEAD_PAD)
        q_ref[:, sl] = _rot(qa[:, sl], cq, sq).astype(BF16)
        k_ref[:, sl] = (kn[:, sl] + kr).astype(BF16)


def _in_proj(x2d, w, seq, tm):
    t = x2d.shape[0]
    nseq = seq // tm
    hw = MLA_HEADS * HEAD_PAD
    const = lambda i: (0, 0)
    row = lambda i: (i, 0)
    pos = lambda i: (i % nseq, 0)
    n_in = w["w_in"].shape[1]
    return pl.pallas_call(
        _in_proj_kernel,
        grid=(t // tm,),
        in_specs=[pl.BlockSpec((tm, D_MODEL), row),
                  pl.BlockSpec((D_MODEL, n_in), const),
                  pl.BlockSpec((1, Q_LORA), const),
                  pl.BlockSpec((1, KV_LORA), const),
                  pl.BlockSpec((Q_LORA, hw), const),
                  pl.BlockSpec((KV_LORA, hw), const),
                  pl.BlockSpec((KV_LORA, hw), const),
                  pl.BlockSpec((1, hw), const),
                  pl.BlockSpec((tm, HEAD_PAD), pos),
                  pl.BlockSpec((tm, HEAD_PAD), pos),
                  pl.BlockSpec((tm, HEAD_PAD), pos),
                  pl.BlockSpec((tm, HEAD_PAD), pos)],
        out_specs=[pl.BlockSpec((tm, hw), row),
                   pl.BlockSpec((tm, hw), row),
                   pl.BlockSpec((tm, hw), row),
                   pl.BlockSpec((tm, SSM_W), row)],
        out_shape=[jax.ShapeDtypeStruct((t, hw), BF16),
                   jax.ShapeDtypeStruct((t, hw), BF16),
                   jax.ShapeDtypeStruct((t, hw), BF16),
                   jax.ShapeDtypeStruct((t, SSM_W), F32)],
        compiler_params=_cparams(("parallel",)),
        name="in_proj",
    )(x2d, w["w_in"], w["g_q"], w["g_kv"], w["w_uq"], w["w_kn"], w["w_v"], w["v_one"],
      w["cos_q"], w["sin_q"], w["cos_k"], w["sin_k"])


def _shift_rows(x, d, down):
    n = x.shape[0]
    r = lax.broadcasted_iota(jnp.int32, x.shape, 0)
    if down:
        return jnp.where(r >= d, pltpu.roll(x, d, axis=0), 0.0)
    return jnp.where(r < n - d, pltpu.roll(x, n - d, axis=0), 0.0)


def _cmul(a_re, a_im, x):
    return a_re * x + a_im * pltpu.roll(x, SSM_P, axis=1)


def _s5_kernel(u_ref, mt_ref, bin_ref, cout_ref, pw_ref, y_ref, *, n_steps):
    for g in range(SSM_GB):
        ub = u_ref[0, g]
        hin = _dot(ub, bin_ref[g])
        hf = hin[:, :2 * SSM_P]
        hb = hin[:, 2 * SSM_P:]
        for s in range(n_steps):
            d = 1 << s
            hf = hf + _cmul(pw_ref[g, s, 0:1, :], pw_ref[g, s, 1:2, :], _shift_rows(hf, d, True))
            hb = hb + _cmul(pw_ref[g, s, 2:3, :], pw_ref[g, s, 3:4, :], _shift_rows(hb, d, False))
        hprev = jnp.concatenate([_shift_rows(hf, 1, True), _shift_rows(hb, 1, False)], axis=1)
        y = _dot(ub, mt_ref[g]) + _dot(hprev.astype(BF16), cout_ref[g])
        y_ref[0, g] = y


def _s5(u_c, w):
    b, g, nc, cw = u_c.shape
    n_steps = max(1, int(math.log2(nc)))
    assert (1 << n_steps) == nc
    pw = w["ssm_pw"][:, :n_steps]
    blk = lambda bi, gi: (bi, gi, 0, 0)
    wblk = lambda bi, gi: (gi, 0, 0)
    return pl.pallas_call(
        functools.partial(_s5_kernel, n_steps=n_steps),
        grid=(b, g // SSM_GB),
        in_specs=[pl.BlockSpec((1, SSM_GB, nc, cw), blk),
                  pl.BlockSpec((SSM_GB, cw, cw), wblk),
                  pl.BlockSpec((SSM_GB, cw, cw), wblk),
                  pl.BlockSpec((SSM_GB, cw, cw), wblk),
                  pl.BlockSpec((SSM_GB, n_steps, 4, 2 * SSM_P), lambda bi, gi: (gi, 0, 0, 0))],
        out_specs=pl.BlockSpec((1, SSM_GB, nc, cw), blk),
        out_shape=jax.ShapeDtypeStruct((b, g, nc, cw), F32),
        compiler_params=_cparams(("parallel", "parallel")),
        name="s5",
    )(u_c, w["ssm_mt"], w["ssm_bin"], w["ssm_cout"], pw)


def _mla_kernel(q_ref, k_ref, v_ref, o_ref, *, tk):
    s_len = k_ref.shape[1]
    tq = q_ref.shape[1]
    outs = []
    for hh in range(2):
        sl = slice(hh * HEAD_PAD, (hh + 1) * HEAD_PAD)
        q = q_ref[0, :, sl]

        def body(j, carry):
            m, acc = carry
            off = pl.multiple_of(j * tk, tk)
            kc = k_ref[0, pl.ds(off, tk), sl]
            vc = v_ref[0, pl.ds(off, tk), sl]
            s = _dot_t(q, kc)
            m_new = jnp.maximum(m, jnp.max(s, axis=-1, keepdims=True))
            p = jnp.exp(s - m_new)
            acc = jnp.exp(m - m_new) * acc + _dot(p.astype(BF16), vc)
            return m_new, acc

        m0 = jnp.full((tq, 1), -jnp.inf, F32)
        a0 = jnp.zeros((tq, HEAD_PAD), F32)
        _, acc = lax.fori_loop(0, s_len // tk, body, (m0, a0))
        outs.append(acc / acc[:, V_DIM:V_DIM + 1])
    lane = lax.broadcasted_iota(jnp.int32, (tq, HEAD_PAD), 1)
    o = jnp.where(lane < V_DIM, outs[0], pltpu.roll(outs[1], V_DIM, axis=1))
    o_ref[0] = o.astype(o_ref.dtype)


def _mla(q, k, v, tq, tk):
    b, s, hw = q.shape
    return pl.pallas_call(
        functools.partial(_mla_kernel, tk=tk),
        grid=(b, MLA_HEADS // 2, s // tq),
        in_specs=[pl.BlockSpec((1, tq, 2 * HEAD_PAD), lambda bi, hi, qi: (bi, qi, hi)),
                  pl.BlockSpec((1, s, 2 * HEAD_PAD), lambda bi, hi, qi: (bi, 0, hi)),
                  pl.BlockSpec((1, s, 2 * HEAD_PAD), lambda bi, hi, qi: (bi, 0, hi))],
        out_specs=pl.BlockSpec((1, tq, 2 * V_DIM), lambda bi, hi, qi: (bi, qi, hi)),
        out_shape=jax.ShapeDtypeStruct((b, s, ATT_W), BF16),
        compiler_params=_cparams(("parallel", "parallel", "arbitrary")),
        name="mla",
    )(q, k, v)


def _split_bf16(x):
    hi = x.astype(BF16)
    lo = (x - hi.astype(F32)).astype(BF16)
    return hi, lo


def _mix_kernel(x_ref, att_ref, y_ref, u_ref, d_ref, w_glu_ref, ga_ref, gs_ref, wo_a_ref, wo_s_ref,
                ln1g_ref, ln1b_ref, w_cq_ref, kc_ref, vc_ref, w_co_ref, ln2g_ref, ln2b_ref,
                wr_hi_ref, wr_lo_ref, br_ref, x2_ref, lg_ref):
    y = y_ref[...] + d_ref[...] * u_ref[...]
    gy = jax.nn.gelu(y)
    ssm = gy * jax.nn.sigmoid(_dot(gy.astype(BF16), w_glu_ref[...]))
    att = att_ref[...].astype(F32)
    mixed = (_dot(_rms(att, ga_ref[...]).astype(BF16), wo_a_ref[...])
             + _dot(_rms(ssm, gs_ref[...]).astype(BF16), wo_s_ref[...]))
    x1 = _ln(ALPHA * x_ref[...] + mixed, ln1g_ref[...], ln1b_ref[...])
    qc = (_dot(x1.astype(BF16), w_cq_ref[...]) * (XA_DIM ** -0.5)).astype(BF16)
    heads = []
    for h in range(XA_HEADS):
        sl = slice(h * XA_DIM, (h + 1) * XA_DIM)
        s = _dot_t(qc[:, sl], kc_ref[0, :, sl])
        p = jnp.exp(s - jnp.max(s, axis=-1, keepdims=True))
        o = _dot(p.astype(BF16), vc_ref[0, :, sl])
        heads.append(o / jnp.sum(p, axis=-1, keepdims=True))
    ca = _dot(jnp.concatenate(heads, axis=1).astype(BF16), w_co_ref[...])
    x2 = _ln(ALPHA * x1 + ca, ln2g_ref[...], ln2b_ref[...])
    x2_ref[...] = x2
    hi, lo = _split_bf16(x2)
    lg_ref[...] = (_dot(hi, wr_hi_ref[...]) + _dot(lo, wr_hi_ref[...]) + _dot(hi, wr_lo_ref[...])
                   + br_ref[...])


def _mix(x2d, att, y, u, kc, vc, w, seq, tm):
    t = x2d.shape[0]
    nseq = seq // tm
    n_mem = kc.shape[1]
    const = lambda i: (0, 0)
    row = lambda i: (i, 0)
    mem = lambda i: (i // nseq, 0, 0)
    vec = lambda n: pl.BlockSpec((1, n), const)
    return pl.pallas_call(
        _mix_kernel,
        grid=(t // tm,),
        in_specs=[pl.BlockSpec((tm, D_MODEL), row),
                  pl.BlockSpec((tm, ATT_W), row),
                  pl.BlockSpec((tm, SSM_W), row),
                  pl.BlockSpec((tm, SSM_W), row),
                  vec(SSM_W),
                  pl.BlockSpec((SSM_W, SSM_W), const),
                  vec(ATT_W), vec(SSM_W),
                  pl.BlockSpec((ATT_W, D_MODEL), const),
                  pl.BlockSpec((SSM_W, D_MODEL), const),
                  vec(D_MODEL), vec(D_MODEL),
                  pl.BlockSpec((D_MODEL, D_MODEL), const),
                  pl.BlockSpec((1, n_mem, D_MODEL), mem),
                  pl.BlockSpec((1, n_mem, D_MODEL), mem),
                  pl.BlockSpec((D_MODEL, D_MODEL), const),
                  vec(D_MODEL), vec(D_MODEL),
                  pl.BlockSpec((D_MODEL, ROUTE_W), const),
                  pl.BlockSpec((D_MODEL, ROUTE_W), const),
                  vec(ROUTE_W)],
        out_specs=[pl.BlockSpec((tm, D_MODEL), row),
                   pl.BlockSpec((tm, ROUTE_W), row)],
        out_shape=[jax.ShapeDtypeStruct((t, D_MODEL), F32),
                   jax.ShapeDtypeStruct((t, ROUTE_W), F32)],
        compiler_params=_cparams(("parallel",)),
        name="mix",
    )(x2d, att, y, u, w["ssm_d"], w["w_glu"], w["g_attn"], w["g_ssm"], w["w_out_a"], w["w_out_s"],
      w["ln1_g"], w["ln1_b"], w["w_cq"], kc, vc, w["w_co"], w["ln2_g"], w["ln2_b"],
      w["wr_hi"], w["wr_lo"], w["b_r"])


def _row_copy(src_hbm, dst_vmem, sem, src_row, dst_row):
    return pltpu.make_async_copy(src_hbm.at[pl.ds(src_row, 1)], dst_vmem.at[pl.ds(dst_row, 1)], sem)


def _experts_kernel(te_ref, nv_ref, src_hbm, x_hbm, gate_ref, w1_ref, w3_ref, w2_ref, y_ref,
                    idx_smem, xbuf, isem, gsem):
    i = pl.program_id(0)
    tm = xbuf.shape[0]

    @pl.when(i < nv_ref[0])
    def _():
        cp = pltpu.make_async_copy(src_hbm.at[i], idx_smem, isem)
        cp.start()
        cp.wait()

        def issue(r, c):
            _row_copy(x_hbm, xbuf, gsem, idx_smem[r], r).start()
            return c

        lax.fori_loop(0, tm, issue, 0)

        def drain(r, c):
            _row_copy(x_hbm, xbuf, gsem, 0, r).wait()
            return c

        lax.fori_loop(0, tm, drain, 0)
        xb = xbuf[...].astype(BF16)
        hdn = jax.nn.silu(_dot(xb, w1_ref[0])) * _dot(xb, w3_ref[0])
        y_ref[...] = gate_ref[...] * _dot(hdn.astype(BF16), w2_ref[0])

    @pl.when(i >= nv_ref[0])
    def _():
        y_ref[...] = jnp.zeros(y_ref.shape, y_ref.dtype)


def _experts(x2, row_src, row_gate, tile_expert, n_valid, w, tm):
    n_tiles = row_src.shape[0]
    ew = lambda i, te, nv: (te[i], 0, 0)
    grid_spec = pltpu.PrefetchScalarGridSpec(
        num_scalar_prefetch=2,
        grid=(n_tiles,),
        in_specs=[pl.BlockSpec(memory_space=pl.ANY),
                  pl.BlockSpec(memory_space=pl.ANY),
                  pl.BlockSpec((tm, 1), lambda i, te, nv: (i, 0)),
                  pl.BlockSpec((1, D_MODEL, D_FF_EXPERT), ew),
                  pl.BlockSpec((1, D_MODEL, D_FF_EXPERT), ew),
                  pl.BlockSpec((1, D_FF_EXPERT, D_MODEL), ew)],
        out_specs=pl.BlockSpec((tm, D_MODEL), lambda i, te, nv: (i, 0)),
        scratch_shapes=[pltpu.SMEM((tm,), jnp.int32),
                        pltpu.VMEM((tm, D_MODEL), F32),
                        pltpu.SemaphoreType.DMA,
                        pltpu.SemaphoreType.DMA],
    )
    return pl.pallas_call(
        _experts_kernel,
        grid_spec=grid_spec,
        out_shape=jax.ShapeDtypeStruct((n_tiles * tm, D_MODEL), F32),
        compiler_params=_cparams(("arbitrary",)),
        name="experts",
    )(tile_expert, n_valid, row_src, x2, row_gate, w["w_e1"], w["w_e3"], w["w_e2"])


def _combine_kernel(pos_hbm, x2_ref, y_hbm, g_ref, b_ref, o_ref, idx_smem, ybuf, isem, gsem):
    i = pl.program_id(0)
    tm = x2_ref.shape[0]
    cp = pltpu.make_async_copy(pos_hbm.at[i], idx_smem, isem)
    cp.start()
    cp.wait()

    def issue(r, c):
        _row_copy(y_hbm, ybuf, gsem, idx_smem[r], r).start()
        return c

    lax.fori_loop(0, TOP_K * tm, issue, 0)

    def drain(r, c):
        _row_copy(y_hbm, ybuf, gsem, 0, r).wait()
        return c

    lax.fori_loop(0, TOP_K * tm, drain, 0)
    moe = ybuf[0:tm, :] + ybuf[tm:2 * tm, :]
    o_ref[...] = _ln(ALPHA * x2_ref[...] + moe, g_ref[...], b_ref[...])


def _combine(x2, y_sorted, pos_tiles, w, tm):
    t = x2.shape[0]
    const = lambda i: (0, 0)
    row = lambda i: (i, 0)
    return pl.pallas_call(
        _combine_kernel,
        grid=(t // tm,),
        in_specs=[pl.BlockSpec(memory_space=pl.ANY),
                  pl.BlockSpec((tm, D_MODEL), row),
                  pl.BlockSpec(memory_space=pl.ANY),
                  pl.BlockSpec((1, D_MODEL), const),
                  pl.BlockSpec((1, D_MODEL), const)],
        out_specs=pl.BlockSpec((tm, D_MODEL), row),
        out_shape=jax.ShapeDtypeStruct((t, D_MODEL), F32),
        scratch_shapes=[pltpu.SMEM((TOP_K * tm,), jnp.int32),
                        pltpu.VMEM((TOP_K * tm, D_MODEL), F32),
                        pltpu.SemaphoreType.DMA,
                        pltpu.SemaphoreType.DMA],
        compiler_params=_cparams(("arbitrary",)),
        name="combine",
    )(pos_tiles, x2, y_sorted, w["ln3_g"], w["ln3_b"])


def _route(logits, tm_e, tm_c):
    t = logits.shape[0]
    lg = logits[:, :N_GROUPS]
    grp = jnp.argmax(lg, axis=-1)
    p_grp = jnp.take_along_axis(jax.nn.softmax(lg, axis=-1), grp[:, None], axis=-1)
    le = logits[:, N_GROUPS:N_GROUPS + N_EXPERTS].reshape(t, N_GROUPS, EXP_PER_GROUP)
    le_sel = jnp.take_along_axis(le, grp[:, None, None], axis=1)[:, 0]
    top_v, top_i = lax.top_k(le_sel, TOP_K)
    w_k = jax.nn.softmax(top_v, axis=-1) * p_grp
    e_id = (grp[:, None] * EXP_PER_GROUP + top_i).astype(jnp.int32)

    flat_e = e_id.reshape(-1)
    n = flat_e.shape[0]
    onehot = (flat_e[:, None] == jnp.arange(N_EXPERTS, dtype=jnp.int32)[None, :]).astype(jnp.int32)
    csum = jnp.cumsum(onehot, axis=0)
    counts = csum[-1]
    rank = jnp.take_along_axis(csum, flat_e[:, None], axis=1)[:, 0] - 1
    padded = ((counts + tm_e - 1) // tm_e) * tm_e
    start_pad = jnp.cumsum(padded) - padded
    start_unp = jnp.cumsum(counts) - counts
    pos = (start_pad[flat_e] + rank).astype(jnp.int32)

    n_tiles = n // tm_e + N_EXPERTS
    rows = jnp.arange(n_tiles * tm_e, dtype=jnp.int32)
    end_pad = start_pad + padded
    row_e = jnp.minimum(jnp.searchsorted(end_pad, rows, side="right"), N_EXPERTS - 1).astype(jnp.int32)
    within = rows - start_pad[row_e]
    valid = within < counts[row_e]
    order = jnp.argsort(flat_e, stable=True).astype(jnp.int32)
    j = jnp.clip(start_unp[row_e] + within, 0, n - 1)
    row_src = jnp.where(valid, order[j] // TOP_K, 0).astype(jnp.int32)
    row_gate = jnp.where(valid, w_k.reshape(-1)[order[j]], 0.0).astype(F32)
    tile_expert = row_e[::tm_e]
    n_valid = (jnp.sum(padded) // tm_e).astype(jnp.int32).reshape(1)
    pos_tiles = pos.reshape(t // tm_c, tm_c, TOP_K).transpose(0, 2, 1).reshape(t // tm_c, TOP_K * tm_c)
    return (row_src.reshape(n_tiles, tm_e), row_gate.reshape(n_tiles * tm_e, 1), tile_expert, n_valid,
            pos_tiles)


def _rope_tables(seq):
    inv = ROPE_THETA ** (-jnp.arange(0, ROPE_DIM, 2, dtype=F32) / ROPE_DIM)
    ang = jnp.arange(seq, dtype=F32)[:, None] * inv[None, :]
    c, s = jnp.cos(ang), jnp.sin(ang)
    z = jnp.zeros((seq, HEAD_PAD - NOPE_DIM - ROPE_DIM), F32)
    cos_t = jnp.concatenate([jnp.ones((seq, NOPE_DIM), F32), c, c, z], axis=1)
    sin_t = jnp.concatenate([jnp.zeros((seq, NOPE_DIM), F32), s, s, z], axis=1)
    return cos_t, sin_t


def _pair_rot(w_rope):
    half = ROPE_DIM // 2
    return jnp.concatenate([-w_rope[..., half:], w_rope[..., :half]], axis=-1)


def _ssm_operators(a_re, a_im, log_dt, b_re, b_im, c_re, c_im, max_steps):
    a = lax.complex(a_re, a_im)
    dt = jnp.exp(log_dt)[..., None]
    adt = a * dt
    a_bar = jnp.exp(adt)
    b_bar = ((a_bar - 1.0) / a)[..., None] * lax.complex(b_re, b_im)
    cm = lax.complex(c_re, c_im)
    taus = jnp.arange(SSM_L + 1, dtype=F32)
    pw = jnp.exp(adt[None] * taus[:, None, None, None])
    kf = jnp.real(jnp.einsum("gcp,tgp,gpd->tgcd", cm[0], pw[:SSM_L, 0], b_bar[0]))
    kb = jnp.real(jnp.einsum("gcp,tgp,gpd->tgcd", cm[1], pw[:SSM_L, 1], b_bar[1]))
    zero = jnp.zeros_like(kf[:1])
    kfull = jnp.concatenate([kb[:0:-1], kf[:1] + kb[:1], kf[1:]], axis=0)
    del zero
    ii = jnp.arange(SSM_L)
    lag = ii[:, None] - ii[None, :] + SSM_L - 1
    mt = kfull[lag]
    mt = mt.transpose(2, 1, 4, 0, 3).reshape(SSM_G, SSM_CW, SSM_CW)
    pf = pw[SSM_L - 1 - ii, 0]
    pb = pw[ii, 1]
    sf = pf[..., None] * b_bar[0][None]
    sb = pb[..., None] * b_bar[1][None]
    bin_ = jnp.concatenate([jnp.real(sf), jnp.imag(sf), jnp.real(sb), jnp.imag(sb)], axis=2)
    bin_ = bin_.transpose(1, 0, 3, 2).reshape(SSM_G, SSM_CW, 4 * SSM_P)
    of = cm[0][None] * pw[1 + ii, 0][:, :, None, :]
    ob = cm[1][None] * pw[SSM_L - ii, 1][:, :, None, :]
    cout = jnp.concatenate([jnp.real(of), -jnp.imag(of), jnp.real(ob), -jnp.imag(ob)], axis=3)
    cout = cout.transpose(1, 3, 0, 2).reshape(SSM_G, 4 * SSM_P, SSM_CW)
    steps = (SSM_L * (2.0 ** jnp.arange(max_steps, dtype=F32)))
    sp = jnp.exp(adt[None] * steps[:, None, None, None])
    re, im = jnp.real(sp), jnp.imag(sp)
    tab = jnp.stack([jnp.concatenate([re[:, 0], re[:, 0]], -1), jnp.concatenate([-im[:, 0], im[:, 0]], -1),
                     jnp.concatenate([re[:, 1], re[:, 1]], -1), jnp.concatenate([-im[:, 1], im[:, 1]], -1)],
                    axis=2)
    tab = tab.transpose(1, 0, 2, 3)
    return mt.astype(BF16), bin_.astype(BF16), cout.astype(BF16), tab.astype(F32)


def _prepare(p, max_chunks):
    w = {}
    w_in = p["w_in"]
    o_kpe = Q_LORA + KV_LORA
    kpe = w_in[:, o_kpe:o_kpe + ROPE_DIM]
    kpe_blk = jnp.concatenate([jnp.zeros((D_MODEL, NOPE_DIM), F32), kpe, _pair_rot(kpe)], axis=1)
    w["w_in"] = jnp.concatenate([w_in[:, :o_kpe], w_in[:, o_kpe + ROPE_DIM:], kpe_blk], axis=1).astype(BF16)
    w["g_q"] = p["g_q_lat"].reshape(1, -1)
    w["g_kv"] = p["g_kv_lat"].reshape(1, -1)
    uq = p["w_uq"].reshape(Q_LORA, MLA_HEADS, QK_DIM)
    rope = uq[..., NOPE_DIM:]
    w["w_uq"] = jnp.concatenate([uq, _pair_rot(rope)], axis=-1).reshape(Q_LORA, -1).astype(BF16)
    ukv = p["w_ukv"].reshape(KV_LORA, MLA_HEADS, NOPE_DIM + V_DIM)
    zpad = jnp.zeros((KV_LORA, MLA_HEADS, HEAD_PAD - NOPE_DIM), F32)
    w["w_kn"] = jnp.concatenate([ukv[..., :NOPE_DIM], zpad], axis=-1).reshape(KV_LORA, -1).astype(BF16)
    w["w_v"] = jnp.concatenate([ukv[..., NOPE_DIM:], zpad], axis=-1).reshape(KV_LORA, -1).astype(BF16)
    one = jnp.zeros((MLA_HEADS, HEAD_PAD), F32).at[:, V_DIM].set(1.0)
    w["v_one"] = one.reshape(1, -1)
    max_steps = max(1, int(math.log2(max_chunks)))
    w["ssm_mt"], w["ssm_bin"], w["ssm_cout"], w["ssm_pw"] = _ssm_operators(
        p["ssm_a_re"], p["ssm_a_im"], p["ssm_log_dt"], p["ssm_b_re"], p["ssm_b_im"],
        p["ssm_c_re"], p["ssm_c_im"], max_steps)
    w["ssm_d"] = p["ssm_d"].reshape(1, -1)
    w["w_glu"] = p["w_glu"].astype(BF16)
    w["g_attn"] = p["g_attn_grp"].reshape(1, -1)
    w["g_ssm"] = p["g_ssm_grp"].reshape(1, -1)
    w["w_out_a"] = p["w_out"][:ATT_W].astype(BF16)
    w["w_out_s"] = p["w_out"][ATT_W:].astype(BF16)
    for n in ("ln1_g", "ln1_b", "ln2_g", "ln2_b", "ln3_g", "ln3_b"):
        w[n] = p[n].reshape(1, -1)
    w["w_cq"] = p["w_cq"].astype(BF16)
    w["w_ckv"] = p["w_ckv"].astype(BF16)
    w["w_co"] = p["w_co"].astype(BF16)
    wr = jnp.concatenate([p["w_rg"], p["w_re"],
                          jnp.zeros((D_MODEL, ROUTE_W - N_GROUPS - N_EXPERTS), F32)], axis=1)
    w["wr_hi"], w["wr_lo"] = _split_bf16(wr)
    w["b_r"] = jnp.concatenate([p["b_rg"], p["b_re"],
                                jnp.zeros((ROUTE_W - N_GROUPS - N_EXPERTS,), F32)]).reshape(1, -1)
    w["w_e1"] = p["w_e1"].astype(BF16)
    w["w_e3"] = p["w_e3"].astype(BF16)
    w["w_e2"] = p["w_e2"].astype(BF16)
    return w


def _tile(n, pref):
    return pref if n % pref == 0 else n


def _run(x, mem, w):
    b, s, _ = x.shape
    t = b * s
    n_mem = mem.shape[1]
    x2d = x.reshape(t, D_MODEL)
    tm = _tile(s, 512)
    scale = QK_DIM ** -0.5
    cos_t, sin_t = _rope_tables(s)
    wl = dict(w, cos_q=cos_t * scale, sin_q=sin_t * scale, cos_k=cos_t, sin_k=sin_t)

    kvc = _matmul(mem.reshape(b * n_mem, D_MODEL), w["w_ckv"], BF16, n_mem, 1024)
    kvc = kvc.reshape(b, n_mem, 2 * D_MODEL)
    kc, vc = kvc[:, :, :D_MODEL], kvc[:, :, D_MODEL:]

    q, k, v, u = _in_proj(x2d, wl, s, tm)
    hw = MLA_HEADS * HEAD_PAD
    att = _mla(q.reshape(b, s, hw), k.reshape(b, s, hw), v.reshape(b, s, hw), _tile(s, 512), _tile(s, 512))

    nc = s // SSM_L
    u_c = (u.astype(BF16).reshape(b, nc, SSM_L, SSM_G, SSM_GC).transpose(0, 3, 1, 2, 4)
           .reshape(b, SSM_G, nc, SSM_CW))
    y_c = _s5(u_c, w)
    y = (y_c.reshape(b, SSM_G, nc, SSM_L, SSM_GC).transpose(0, 2, 3, 1, 4).reshape(t, SSM_W))

    tmx = _tile(s, 256)
    x2, logits = _mix(x2d, att.reshape(t, ATT_W), y, u, kc, vc, w, s, tmx)

    tm_e = 256
    tm_c = _tile(t, 256)
    row_src, row_gate, tile_expert, n_valid, pos_tiles = _route(logits, tm_e, tm_c)
    y_sorted = _experts(x2, row_src, row_gate, tile_expert, n_valid, w, tm_e)
    out = _combine(x2, y_sorted, pos_tiles, w, tm_c)
    return out.reshape(b, s, D_MODEL)


def kernel(x_prompt, x_sample, mem_prompt, mem_sample, w_in, g_q_lat, w_uq, g_kv_lat, w_ukv, ssm_a_re, ssm_a_im, ssm_log_dt, ssm_b_re, ssm_b_im, ssm_c_re, ssm_c_im, ssm_d, w_glu, g_attn_grp, g_ssm_grp, w_out, ln1_g, ln1_b, w_cq, w_ckv, w_co, ln2_g, ln2_b, w_rg, b_rg, w_re, b_re, w_e1, w_e3, w_e2, ln3_g, ln3_b):
    params = dict(w_in=w_in, g_q_lat=g_q_lat, w_uq=w_uq, g_kv_lat=g_kv_lat, w_ukv=w_ukv,
                  ssm_a_re=ssm_a_re, ssm_a_im=ssm_a_im, ssm_log_dt=ssm_log_dt, ssm_b_re=ssm_b_re,
                  ssm_b_im=ssm_b_im, ssm_c_re=ssm_c_re, ssm_c_im=ssm_c_im, ssm_d=ssm_d, w_glu=w_glu,
                  g_attn_grp=g_attn_grp, g_ssm_grp=g_ssm_grp, w_out=w_out, ln1_g=ln1_g, ln1_b=ln1_b,
                  w_cq=w_cq, w_ckv=w_ckv, w_co=w_co, ln2_g=ln2_g, ln2_b=ln2_b,
                  w_rg=w_rg, b_rg=b_rg, w_re=w_re, b_re=b_re, w_e1=w_e1, w_e3=w_e3, w_e2=w_e2,
                  ln3_g=ln3_g, ln3_b=ln3_b)
    params = {name: val[0] for name, val in params.items()}
    max_chunks = max(x_prompt.shape[1], x_sample.shape[1]) // SSM_L
    w = _prepare(params, max_chunks)
    return (_run(x_prompt, mem_prompt, w), _run(x_sample, mem_sample, w))
```

```python
import functools
import math

import jax
import jax.numpy as jnp
from jax import lax
from jax.experimental import pallas as pl
from jax.experimental.pallas import tpu as pltpu

F32 = jnp.float32
BF16 = jnp.bfloat16

D_MODEL = 1024
MLA_HEADS = 8
NOPE_DIM = 64
ROPE_DIM = 32
V_DIM = 64
QK_DIM = NOPE_DIM + ROPE_DIM
Q_LORA = 384
KV_LORA = 256
ATT_W = MLA_HEADS * V_DIM
ROPE_THETA = 10000.0
SSM_W = D_MODEL - ATT_W
SSM_GC = 16
SSM_G = SSM_W // SSM_GC
SSM_P = 64
XA_HEADS = 4
XA_DIM = D_MODEL // XA_HEADS
N_GROUPS = 4
EXP_PER_GROUP = 8
N_EXPERTS = N_GROUPS * EXP_PER_GROUP
TOP_K = 2
D_FF_EXPERT = 256
DEPTH = 1
ALPHA = (2 * DEPTH) ** 0.25
LN_EPS = 1e-5
RMS_EPS = 1e-6

LANE = 128
HEAD_PAD = 128
SSM_L = 16
SSM_CW = SSM_L * SSM_GC
SSM_GB = 4
ROUTE_W = 128
ROW_SUB = D_MODEL // LANE
GATHER_UNROLL = 8
VMEM_LIMIT = 56 * 1024 * 1024


def _cparams(sem):
    return pltpu.CompilerParams(dimension_semantics=sem, vmem_limit_bytes=VMEM_LIMIT)


def _rms(x, g):
    return x * lax.rsqrt(jnp.mean(x * x, axis=-1, keepdims=True) + RMS_EPS) * g


def _ln(x, g, b):
    mu = jnp.mean(x, axis=-1, keepdims=True)
    xc = x - mu
    var = jnp.mean(xc * xc, axis=-1, keepdims=True)
    return xc * lax.rsqrt(var + LN_EPS) * g + b


def _dot(a, b):
    return jnp.dot(a, b, preferred_element_type=F32)


def _dot_t(a, b):
    return lax.dot_general(a, b, (((1,), (1,)), ((), ())), preferred_element_type=F32)


def _matmul_kernel(x_ref, w_ref, o_ref):
    o_ref[...] = _dot(x_ref[...].astype(BF16), w_ref[...]).astype(o_ref.dtype)


def _matmul(x, w, out_dtype, tm, tn):
    m, k = x.shape
    n = w.shape[1]
    return pl.pallas_call(
        _matmul_kernel,
        grid=(m // tm, n // tn),
        in_specs=[pl.BlockSpec((tm, k), lambda i, j: (i, 0)),
                  pl.BlockSpec((k, tn), lambda i, j: (0, j))],
        out_specs=pl.BlockSpec((tm, tn), lambda i, j: (i, j)),
        out_shape=jax.ShapeDtypeStruct((m, n), out_dtype),
        compiler_params=_cparams(("parallel", "parallel")),
        name="mem_kv",
    )(x, w)


def _rot(blk, cos, sin):
    return blk * cos + pltpu.roll(blk, HEAD_PAD - ROPE_DIM, axis=1) * sin


def _in_proj_kernel(x_ref, w_in_ref, gq_ref, gkv_ref, w_uq_ref, w_kn_ref, w_v_ref, vone_ref,
                    cq_ref, sq_ref, ck_ref, sk_ref,
                    q_ref, k_ref, v_ref, u_ref):
    xb = x_ref[...].astype(BF16)
    proj = _dot(xb, w_in_ref[...])
    o_kv = Q_LORA
    o_u = Q_LORA + KV_LORA
    o_kr = o_u + SSM_W
    u_ref[...] = proj[:, o_u:o_kr]
    qn = _rms(proj[:, :Q_LORA], gq_ref[...]).astype(BF16)
    kvn = _rms(proj[:, o_kv:o_u], gkv_ref[...]).astype(BF16)
    qa = _dot(qn, w_uq_ref[...])
    kn = _dot(kvn, w_kn_ref[...])
    v = _dot(kvn, w_v_ref[...]) + vone_ref[...]
    v_ref[...] = v.astype(BF16)
    cq, sq, ck, sk = cq_ref[...], sq_ref[...], ck_ref[...], sk_ref[...]
    kr = _rot(proj[:, o_kr:o_kr + HEAD_PAD], ck, sk)
    for h in range(MLA_HEADS):
        sl = slice(h * HEAD_PAD, (h + 1) * HEAD_PAD)
        q_ref[:, sl] = _rot(qa[:, sl], cq, sq).astype(BF16)
        k_ref[:, sl] = (kn[:, sl] + kr).astype(BF16)


def _in_proj(x2d, w, seq, tm):
    t = x2d.shape[0]
    nseq = seq // tm
    hw = MLA_HEADS * HEAD_PAD
    const = lambda i: (0, 0)
    row = lambda i: (i, 0)
    pos = lambda i: (i % nseq, 0)
    n_in = w["w_in"].shape[1]
    return pl.pallas_call(
        _in_proj_kernel,
        grid=(t // tm,),
        in_specs=[pl.BlockSpec((tm, D_MODEL), row),
                  pl.BlockSpec((D_MODEL, n_in), const),
                  pl.BlockSpec((1, Q_LORA), const),
                  pl.BlockSpec((1, KV_LORA), const),
                  pl.BlockSpec((Q_LORA, hw), const),
                  pl.BlockSpec((KV_LORA, hw), const),
                  pl.BlockSpec((KV_LORA, hw), const),
                  pl.BlockSpec((1, hw), const),
                  pl.BlockSpec((tm, HEAD_PAD), pos),
                  pl.BlockSpec((tm, HEAD_PAD), pos),
                  pl.BlockSpec((tm, HEAD_PAD), pos),
                  pl.BlockSpec((tm, HEAD_PAD), pos)],
        out_specs=[pl.BlockSpec((tm, hw), row),
                   pl.BlockSpec((tm, hw), row),
                   pl.BlockSpec((tm, hw), row),
                   pl.BlockSpec((tm, SSM_W), row)],
        out_shape=[jax.ShapeDtypeStruct((t, hw), BF16),
                   jax.ShapeDtypeStruct((t, hw), BF16),
                   jax.ShapeDtypeStruct((t, hw), BF16),
                   jax.ShapeDtypeStruct((t, SSM_W), F32)],
        compiler_params=_cparams(("parallel",)),
        name="in_proj",
    )(x2d, w["w_in"], w["g_q"], w["g_kv"], w["w_uq"], w["w_kn"], w["w_v"], w["v_one"],
      w["cos_q"], w["sin_q"], w["cos_k"], w["sin_k"])


def _shift_rows(x, d, down):
    n = x.shape[0]
    r = lax.broadcasted_iota(jnp.int32, x.shape, 0)
    if down:
        return jnp.where(r >= d, pltpu.roll(x, d, axis=0), 0.0)
    return jnp.where(r < n - d, pltpu.roll(x, n - d, axis=0), 0.0)


def _cmul(a_re, a_im, x):
    return a_re * x + a_im * pltpu.roll(x, SSM_P, axis=1)


def _s5_kernel(u_ref, mt_ref, bin_ref, cout_ref, pw_ref, y_ref, *, n_steps):
    for g in range(SSM_GB):
        ub = u_ref[0, g]
        hin = _dot(ub, bin_ref[g])
        hf = hin[:, :2 * SSM_P]
        hb = hin[:, 2 * SSM_P:]
        for s in range(n_steps):
            d = 1 << s
            hf = hf + _cmul(pw_ref[g, s, 0:1, :], pw_ref[g, s, 1:2, :], _shift_rows(hf, d, True))
            hb = hb + _cmul(pw_ref[g, s, 2:3, :], pw_ref[g, s, 3:4, :], _shift_rows(hb, d, False))
        hprev = jnp.concatenate([_shift_rows(hf, 1, True), _shift_rows(hb, 1, False)], axis=1)
        y = _dot(ub, mt_ref[g]) + _dot(hprev.astype(BF16), cout_ref[g])
        y_ref[0, g] = y


def _s5(u_c, w):
    b, g, nc, cw = u_c.shape
    n_steps = max(1, int(math.log2(nc)))
    assert (1 << n_steps) == nc
    pw = w["ssm_pw"][:, :n_steps]
    blk = lambda bi, gi: (bi, gi, 0, 0)
    wblk = lambda bi, gi: (gi, 0, 0)
    return pl.pallas_call(
        functools.partial(_s5_kernel, n_steps=n_steps),
        grid=(b, g // SSM_GB),
        in_specs=[pl.BlockSpec((1, SSM_GB, nc, cw), blk),
                  pl.BlockSpec((SSM_GB, cw, cw), wblk),
                  pl.BlockSpec((SSM_GB, cw, cw), wblk),
                  pl.BlockSpec((SSM_GB, cw, cw), wblk),
                  pl.BlockSpec((SSM_GB, n_steps, 4, 2 * SSM_P), lambda bi, gi: (gi, 0, 0, 0))],
        out_specs=pl.BlockSpec((1, SSM_GB, nc, cw), blk),
        out_shape=jax.ShapeDtypeStruct((b, g, nc, cw), F32),
        compiler_params=_cparams(("parallel", "parallel")),
        name="s5",
    )(u_c, w["ssm_mt"], w["ssm_bin"], w["ssm_cout"], pw)


def _mla_kernel(q_ref, k_ref, v_ref, o_ref, *, tk):
    s_len = k_ref.shape[1]
    tq = q_ref.shape[1]
    qs = [q_ref[0, :, hh * HEAD_PAD:(hh + 1) * HEAD_PAD] for hh in range(2)]

    def body(j, carry):
        off = pl.multiple_of(j * tk, tk)
        new = []
        for hh in range(2):
            m, acc = carry[hh]
            sl = slice(hh * HEAD_PAD, (hh + 1) * HEAD_PAD)
            s = _dot_t(qs[hh], k_ref[0, pl.ds(off, tk), sl])
            m_new = jnp.maximum(m, jnp.max(s, axis=-1, keepdims=True))
            p = jnp.exp2(s - m_new)
            acc = jnp.exp2(m - m_new) * acc + _dot(p.astype(BF16), v_ref[0, pl.ds(off, tk), sl])
            new.append((m_new, acc))
        return tuple(new)

    init = tuple((jnp.full((tq, 1), -jnp.inf, F32), jnp.zeros((tq, HEAD_PAD), F32)) for _ in range(2))
    res = lax.fori_loop(0, s_len // tk, body, init, unroll=True)
    outs = [acc / acc[:, V_DIM:V_DIM + 1] for _, acc in res]
    lane = lax.broadcasted_iota(jnp.int32, (tq, HEAD_PAD), 1)
    o = jnp.where(lane < V_DIM, outs[0], pltpu.roll(outs[1], V_DIM, axis=1))
    o_ref[0] = o.astype(o_ref.dtype)


def _mla(q, k, v, tq, tk):
    b, s, hw = q.shape
    return pl.pallas_call(
        functools.partial(_mla_kernel, tk=tk),
        grid=(b, MLA_HEADS // 2, s // tq),
        in_specs=[pl.BlockSpec((1, tq, 2 * HEAD_PAD), lambda bi, hi, qi: (bi, qi, hi)),
                  pl.BlockSpec((1, s, 2 * HEAD_PAD), lambda bi, hi, qi: (bi, 0, hi)),
                  pl.BlockSpec((1, s, 2 * HEAD_PAD), lambda bi, hi, qi: (bi, 0, hi))],
        out_specs=pl.BlockSpec((1, tq, 2 * V_DIM), lambda bi, hi, qi: (bi, qi, hi)),
        out_shape=jax.ShapeDtypeStruct((b, s, ATT_W), BF16),
        compiler_params=_cparams(("parallel", "parallel", "arbitrary")),
        name="mla",
    )(q, k, v)


def _split_bf16(x):
    hi = x.astype(BF16)
    lo = (x - hi.astype(F32)).astype(BF16)
    return hi, lo


def _mix_kernel(x_ref, att_ref, y_ref, u_ref, d_ref, w_glu_ref, ga_ref, gs_ref, wo_a_ref, wo_s_ref,
                ln1g_ref, ln1b_ref, w_cq_ref, kc_ref, vc_ref, w_co_ref, ln2g_ref, ln2b_ref,
                wr_hi_ref, wr_lo_ref, br_ref, x2_ref, lg_ref):
    y = y_ref[...] + d_ref[...] * u_ref[...]
    gy = jax.nn.gelu(y)
    ssm = gy * jax.nn.sigmoid(_dot(gy.astype(BF16), w_glu_ref[...]))
    att = att_ref[...].astype(F32)
    mixed = (_dot(_rms(att, ga_ref[...]).astype(BF16), wo_a_ref[...])
             + _dot(_rms(ssm, gs_ref[...]).astype(BF16), wo_s_ref[...]))
    x1 = _ln(ALPHA * x_ref[...] + mixed, ln1g_ref[...], ln1b_ref[...])
    qc = (_dot(x1.astype(BF16), w_cq_ref[...]) * (XA_DIM ** -0.5)).astype(BF16)
    heads = []
    for h in range(XA_HEADS):
        sl = slice(h * XA_DIM, (h + 1) * XA_DIM)
        s = _dot_t(qc[:, sl], kc_ref[0, :, sl])
        p = jnp.exp(s - jnp.max(s, axis=-1, keepdims=True))
        o = _dot(p.astype(BF16), vc_ref[0, :, sl])
        heads.append(o / jnp.sum(p, axis=-1, keepdims=True))
    ca = _dot(jnp.concatenate(heads, axis=1).astype(BF16), w_co_ref[...])
    x2 = _ln(ALPHA * x1 + ca, ln2g_ref[...], ln2b_ref[...])
    _store_rows(x2_ref, x2)
    hi, lo = _split_bf16(x2)
    logits = (_dot(hi, wr_hi_ref[...]) + _dot(lo, wr_hi_ref[...]) + _dot(hi, wr_lo_ref[...])
              + br_ref[...])
    lg_ref[...] = _gates(logits)


def _first_max(x, lane):
    v = jnp.max(x, axis=-1, keepdims=True)
    i = jnp.min(jnp.where(x == v, lane, ROUTE_W), axis=-1, keepdims=True)
    return v, i


def _gates(logits):
    lane = lax.broadcasted_iota(jnp.int32, logits.shape, 1)
    is_grp = lane < N_GROUPS
    lg = jnp.where(is_grp, logits, -jnp.inf)
    g_max, grp = _first_max(lg, lane)
    p_grp = 1.0 / jnp.sum(jnp.where(is_grp, jnp.exp(lg - g_max), 0.0), axis=-1, keepdims=True)
    e_lane = lane - N_GROUPS
    in_grp = (e_lane >= 0) & (e_lane < N_EXPERTS) & ((e_lane // EXP_PER_GROUP) == grp)
    le = jnp.where(in_grp, logits, -jnp.inf)
    v1, i1 = _first_max(le, lane)
    v2, i2 = _first_max(jnp.where(lane == i1, -jnp.inf, le), lane)
    e21 = jnp.exp(v2 - v1)
    w1 = p_grp / (1.0 + e21)
    w2 = w1 * e21
    out = jnp.where(lane == 0, (i1 - N_GROUPS).astype(F32),
                    jnp.where(lane == 1, (i2 - N_GROUPS).astype(F32),
                              jnp.where(lane == 2, w1, jnp.where(lane == 3, w2, 0.0))))
    return out


def _mix(x2d, att, y, u, kc, vc, w, seq, tm):
    t = x2d.shape[0]
    nseq = seq // tm
    n_mem = kc.shape[1]
    const = lambda i: (0, 0)
    row = lambda i: (i, 0)
    mem = lambda i: (i // nseq, 0, 0)
    vec = lambda n: pl.BlockSpec((1, n), const)
    return pl.pallas_call(
        _mix_kernel,
        grid=(t // tm,),
        in_specs=[pl.BlockSpec((tm, D_MODEL), row),
                  pl.BlockSpec((tm, ATT_W), row),
                  pl.BlockSpec((tm, SSM_W), row),
                  pl.BlockSpec((tm, SSM_W), row),
                  vec(SSM_W),
                  pl.BlockSpec((SSM_W, SSM_W), const),
                  vec(ATT_W), vec(SSM_W),
                  pl.BlockSpec((ATT_W, D_MODEL), const),
                  pl.BlockSpec((SSM_W, D_MODEL), const),
                  vec(D_MODEL), vec(D_MODEL),
                  pl.BlockSpec((D_MODEL, D_MODEL), const),
                  pl.BlockSpec((1, n_mem, D_MODEL), mem),
                  pl.BlockSpec((1, n_mem, D_MODEL), mem),
                  pl.BlockSpec((D_MODEL, D_MODEL), const),
                  vec(D_MODEL), vec(D_MODEL),
                  pl.BlockSpec((D_MODEL, ROUTE_W), const),
                  pl.BlockSpec((D_MODEL, ROUTE_W), const),
                  vec(ROUTE_W)],
        out_specs=[pl.BlockSpec((tm * ROW_SUB, LANE), row),
                   pl.BlockSpec((tm, ROUTE_W), row)],
        out_shape=[jax.ShapeDtypeStruct((t * ROW_SUB, LANE), F32),
                   jax.ShapeDtypeStruct((t, ROUTE_W), F32)],
        compiler_params=_cparams(("parallel",)),
        name="mix",
    )(x2d, att, y, u, w["ssm_d"], w["w_glu"], w["g_attn"], w["g_ssm"], w["w_out_a"], w["w_out_s"],
      w["ln1_g"], w["ln1_b"], w["w_cq"], kc, vc, w["w_co"], w["ln2_g"], w["ln2_b"],
      w["wr_hi"], w["wr_lo"], w["b_r"])


def _row_copy(src_hbm, dst_vmem, sem, src_row, dst_row):
    return pltpu.make_async_copy(src_hbm.at[pl.ds(pl.multiple_of(src_row * ROW_SUB, ROW_SUB), ROW_SUB)],
                                 dst_vmem.at[pl.ds(pl.multiple_of(dst_row * ROW_SUB, ROW_SUB), ROW_SUB)], sem)


def _load_rows(ref, n_rows):
    return jnp.concatenate([ref[pl.ds(j, n_rows, stride=ROW_SUB), :] for j in range(ROW_SUB)], axis=1)


def _store_rows(ref, val):
    n_rows = val.shape[0]
    for j in range(ROW_SUB):
        ref[pl.ds(j, n_rows, stride=ROW_SUB), :] = val[:, j * LANE:(j + 1) * LANE]


def _gather_rows(src_hbm, idx_smem, dst_vmem, sem, n_rows):
    def issue(r, c):
        _row_copy(src_hbm, dst_vmem, sem, idx_smem[r], r).start()
        return c

    lax.fori_loop(0, n_rows, issue, 0, unroll=GATHER_UNROLL)


def _wait_rows(src_hbm, dst_vmem, sem, n_rows):
    def drain(r, c):
        _row_copy(src_hbm, dst_vmem, sem, 0, r).wait()
        return c

    lax.fori_loop(0, n_rows, drain, 0, unroll=GATHER_UNROLL)


def _fetch_tile(i, idx_hbm, idx_smem, isem, src_hbm, buf, gsem, n_rows):
    cp = pltpu.make_async_copy(idx_hbm.at[i], idx_smem, isem)
    cp.start()
    cp.wait()
    _gather_rows(src_hbm, idx_smem, buf, gsem, n_rows)
    _wait_rows(src_hbm, buf, gsem, n_rows)


def _experts_kernel(te_ref, nv_ref, src_hbm, x_hbm, gate_ref, w1_ref, w3_ref, w2_ref, y_ref,
                    idx_smem, xbuf, isem, gsem):
    i = pl.program_id(0)
    n = nv_ref[0]
    tm = xbuf.shape[0] // ROW_SUB

    @pl.when(i < n)
    def _():
        _fetch_tile(i, src_hbm, idx_smem, isem, x_hbm, xbuf, gsem, tm)
        xb = _load_rows(xbuf, tm).astype(BF16)
        hdn = jax.nn.silu(_dot(xb, w1_ref[0])) * _dot(xb, w3_ref[0])
        _store_rows(y_ref, gate_ref[...] * _dot(hdn.astype(BF16), w2_ref[0]))

    @pl.when(i >= n)
    def _():
        y_ref[...] = jnp.zeros(y_ref.shape, y_ref.dtype)


def _experts(x2, row_src, row_gate, tile_expert, n_valid, w, tm):
    n_tiles = row_src.shape[0]
    ew = lambda i, te, nv: (te[i], 0, 0)
    grid_spec = pltpu.PrefetchScalarGridSpec(
        num_scalar_prefetch=2,
        grid=(n_tiles,),
        in_specs=[pl.BlockSpec(memory_space=pl.ANY),
                  pl.BlockSpec(memory_space=pl.ANY),
                  pl.BlockSpec((tm, 1), lambda i, te, nv: (i, 0)),
                  pl.BlockSpec((1, D_MODEL, D_FF_EXPERT), ew),
                  pl.BlockSpec((1, D_MODEL, D_FF_EXPERT), ew),
                  pl.BlockSpec((1, D_FF_EXPERT, D_MODEL), ew)],
        out_specs=pl.BlockSpec((tm * ROW_SUB, LANE), lambda i, te, nv: (i, 0)),
        scratch_shapes=[pltpu.SMEM((tm,), jnp.int32),
                        pltpu.VMEM((tm * ROW_SUB, LANE), F32),
                        pltpu.SemaphoreType.DMA,
                        pltpu.SemaphoreType.DMA],
    )
    return pl.pallas_call(
        _experts_kernel,
        grid_spec=grid_spec,
        out_shape=jax.ShapeDtypeStruct((n_tiles * tm * ROW_SUB, LANE), F32),
        compiler_params=_cparams(("arbitrary",)),
        name="experts",
    )(tile_expert, n_valid, row_src, x2, row_gate, w["w_e1"], w["w_e3"], w["w_e2"])


def _combine_kernel(pos_hbm, x2_ref, y_hbm, g_ref, b_ref, o_ref, idx_smem, ybuf, isem, gsem):
    i = pl.program_id(0)
    tm = o_ref.shape[0]
    _fetch_tile(i, pos_hbm, idx_smem, isem, y_hbm, ybuf, gsem, TOP_K * tm)
    rows = _load_rows(ybuf, TOP_K * tm)
    moe = rows[0:tm] + rows[tm:2 * tm]
    o_ref[...] = _ln(ALPHA * _load_rows(x2_ref, tm) + moe, g_ref[...], b_ref[...])


def _combine(x2, y_sorted, pos_tiles, w, tm):
    t = x2.shape[0] // ROW_SUB
    const = lambda i: (0, 0)
    row = lambda i: (i, 0)
    return pl.pallas_call(
        _combine_kernel,
        grid=(t // tm,),
        in_specs=[pl.BlockSpec(memory_space=pl.ANY),
                  pl.BlockSpec((tm * ROW_SUB, LANE), row),
                  pl.BlockSpec(memory_space=pl.ANY),
                  pl.BlockSpec((1, D_MODEL), const),
                  pl.BlockSpec((1, D_MODEL), const)],
        out_specs=pl.BlockSpec((tm, D_MODEL), row),
        out_shape=jax.ShapeDtypeStruct((t, D_MODEL), F32),
        scratch_shapes=[pltpu.SMEM((TOP_K * tm,), jnp.int32),
                        pltpu.VMEM((TOP_K * tm * ROW_SUB, LANE), F32),
                        pltpu.SemaphoreType.DMA,
                        pltpu.SemaphoreType.DMA],
        compiler_params=_cparams(("arbitrary",)),
        name="combine",
    )(pos_tiles, x2, y_sorted, w["ln3_g"], w["ln3_b"])


def _route(gates, tm_e, tm_c):
    t = gates.shape[0]
    flat_e = gates[:, :TOP_K].astype(jnp.int32).reshape(-1)
    flat_w = gates[:, TOP_K:2 * TOP_K].reshape(-1)
    n = flat_e.shape[0]
    ids = jnp.arange(n, dtype=jnp.int32)
    sorted_e, order = lax.sort((flat_e, ids), num_keys=1, is_stable=True)
    _, inv = lax.sort((order, ids), num_keys=1)
    start_unp = jnp.searchsorted(sorted_e, jnp.arange(N_EXPERTS + 1, dtype=jnp.int32), side="left").astype(jnp.int32)
    counts = start_unp[1:] - start_unp[:-1]
    start_unp = start_unp[:-1]
    padded = ((counts + tm_e - 1) // tm_e) * tm_e
    start_pad = jnp.cumsum(padded) - padded
    pos = (start_pad[flat_e] + inv - start_unp[flat_e]).astype(jnp.int32)

    n_tiles = n // tm_e + N_EXPERTS
    rows = jnp.arange(n_tiles * tm_e, dtype=jnp.int32)
    end_pad = start_pad + padded
    row_e = jnp.minimum(jnp.searchsorted(end_pad, rows, side="right"), N_EXPERTS - 1).astype(jnp.int32)
    within = rows - start_pad[row_e]
    valid = within < counts[row_e]
    src = order[jnp.clip(start_unp[row_e] + within, 0, n - 1)]
    row_src = jnp.where(valid, src // TOP_K, 0).astype(jnp.int32)
    row_gate = jnp.where(valid, flat_w[src], 0.0).astype(F32)
    tile_expert = row_e[::tm_e]
    n_valid = (jnp.sum(padded) // tm_e).astype(jnp.int32).reshape(1)
    pos_tiles = pos.reshape(t // tm_c, tm_c, TOP_K).transpose(0, 2, 1).reshape(t // tm_c, TOP_K * tm_c)
    return (row_src.reshape(n_tiles, tm_e), row_gate.reshape(n_tiles * tm_e, 1), tile_expert, n_valid,
            pos_tiles)


def _rope_tables(seq):
    inv = ROPE_THETA ** (-jnp.arange(0, ROPE_DIM, 2, dtype=F32) / ROPE_DIM)
    ang = jnp.arange(seq, dtype=F32)[:, None] * inv[None, :]
    c, s = jnp.cos(ang), jnp.sin(ang)
    z = jnp.zeros((seq, HEAD_PAD - NOPE_DIM - ROPE_DIM), F32)
    cos_t = jnp.concatenate([jnp.ones((seq, NOPE_DIM), F32), c, c, z], axis=1)
    sin_t = jnp.concatenate([jnp.zeros((seq, NOPE_DIM), F32), s, s, z], axis=1)
    return cos_t, sin_t


def _pair_rot(w_rope):
    half = ROPE_DIM // 2
    return jnp.concatenate([-w_rope[..., half:], w_rope[..., :half]], axis=-1)


def _ssm_operators(a_re, a_im, log_dt, b_re, b_im, c_re, c_im, max_steps):
    a = lax.complex(a_re, a_im)
    dt = jnp.exp(log_dt)[..., None]
    adt = a * dt
    a_bar = jnp.exp(adt)
    b_bar = ((a_bar - 1.0) / a)[..., None] * lax.complex(b_re, b_im)
    cm = lax.complex(c_re, c_im)
    taus = jnp.arange(SSM_L + 1, dtype=F32)
    pw = jnp.exp(adt[None] * taus[:, None, None, None])
    kf = jnp.real(jnp.einsum("gcp,tgp,gpd->tgcd", cm[0], pw[:SSM_L, 0], b_bar[0]))
    kb = jnp.real(jnp.einsum("gcp,tgp,gpd->tgcd", cm[1], pw[:SSM_L, 1], b_bar[1]))
    kfull = jnp.concatenate([kb[:0:-1], kf[:1] + kb[:1], kf[1:]], axis=0)
    ii = jnp.arange(SSM_L)
    lag = ii[:, None] - ii[None, :] + SSM_L - 1
    mt = kfull[lag]
    mt = mt.transpose(2, 1, 4, 0, 3).reshape(SSM_G, SSM_CW, SSM_CW)
    pf = pw[SSM_L - 1 - ii, 0]
    pb = pw[ii, 1]
    sf = pf[..., None] * b_bar[0][None]
    sb = pb[..., None] * b_bar[1][None]
    bin_ = jnp.concatenate([jnp.real(sf), jnp.imag(sf), jnp.real(sb), jnp.imag(sb)], axis=2)
    bin_ = bin_.transpose(1, 0, 3, 2).reshape(SSM_G, SSM_CW, 4 * SSM_P)
    of = cm[0][None] * pw[1 + ii, 0][:, :, None, :]
    ob = cm[1][None] * pw[SSM_L - ii, 1][:, :, None, :]
    cout = jnp.concatenate([jnp.real(of), -jnp.imag(of), jnp.real(ob), -jnp.imag(ob)], axis=3)
    cout = cout.transpose(1, 3, 0, 2).reshape(SSM_G, 4 * SSM_P, SSM_CW)
    steps = (SSM_L * (2.0 ** jnp.arange(max_steps, dtype=F32)))
    sp = jnp.exp(adt[None] * steps[:, None, None, None])
    re, im = jnp.real(sp), jnp.imag(sp)
    tab = jnp.stack([jnp.concatenate([re[:, 0], re[:, 0]], -1), jnp.concatenate([-im[:, 0], im[:, 0]], -1),
                     jnp.concatenate([re[:, 1], re[:, 1]], -1), jnp.concatenate([-im[:, 1], im[:, 1]], -1)],
                    axis=2)
    tab = tab.transpose(1, 0, 2, 3)
    return mt.astype(BF16), bin_.astype(BF16), cout.astype(BF16), tab.astype(F32)


def _prepare(p, max_chunks):
    w = {}
    w_in = p["w_in"]
    o_kpe = Q_LORA + KV_LORA
    kpe = w_in[:, o_kpe:o_kpe + ROPE_DIM]
    kpe_blk = jnp.concatenate([jnp.zeros((D_MODEL, NOPE_DIM), F32), kpe, _pair_rot(kpe)], axis=1)
    w["w_in"] = jnp.concatenate([w_in[:, :o_kpe], w_in[:, o_kpe + ROPE_DIM:], kpe_blk], axis=1).astype(BF16)
    w["g_q"] = p["g_q_lat"].reshape(1, -1)
    w["g_kv"] = p["g_kv_lat"].reshape(1, -1)
    uq = p["w_uq"].reshape(Q_LORA, MLA_HEADS, QK_DIM)
    rope = uq[..., NOPE_DIM:]
    w["w_uq"] = jnp.concatenate([uq, _pair_rot(rope)], axis=-1).reshape(Q_LORA, -1).astype(BF16)
    ukv = p["w_ukv"].reshape(KV_LORA, MLA_HEADS, NOPE_DIM + V_DIM)
    zpad = jnp.zeros((KV_LORA, MLA_HEADS, HEAD_PAD - NOPE_DIM), F32)
    w["w_kn"] = jnp.concatenate([ukv[..., :NOPE_DIM], zpad], axis=-1).reshape(KV_LORA, -1).astype(BF16)
    w["w_v"] = jnp.concatenate([ukv[..., NOPE_DIM:], zpad], axis=-1).reshape(KV_LORA, -1).astype(BF16)
    one = jnp.zeros((MLA_HEADS, HEAD_PAD), F32).at[:, V_DIM].set(1.0)
    w["v_one"] = one.reshape(1, -1)
    max_steps = max(1, int(math.log2(max_chunks)))
    w["ssm_mt"], w["ssm_bin"], w["ssm_cout"], w["ssm_pw"] = _ssm_operators(
        p["ssm_a_re"], p["ssm_a_im"], p["ssm_log_dt"], p["ssm_b_re"], p["ssm_b_im"],
        p["ssm_c_re"], p["ssm_c_im"], max_steps)
    w["ssm_d"] = p["ssm_d"].reshape(1, -1)
    w["w_glu"] = p["w_glu"].astype(BF16)
    w["g_attn"] = p["g_attn_grp"].reshape(1, -1)
    w["g_ssm"] = p["g_ssm_grp"].reshape(1, -1)
    w["w_out_a"] = p["w_out"][:ATT_W].astype(BF16)
    w["w_out_s"] = p["w_out"][ATT_W:].astype(BF16)
    for n in ("ln1_g", "ln1_b", "ln2_g", "ln2_b", "ln3_g", "ln3_b"):
        w[n] = p[n].reshape(1, -1)
    w["w_cq"] = p["w_cq"].astype(BF16)
    w["w_ckv"] = p["w_ckv"].astype(BF16)
    w["w_co"] = p["w_co"].astype(BF16)
    wr = jnp.concatenate([p["w_rg"], p["w_re"],
                          jnp.zeros((D_MODEL, ROUTE_W - N_GROUPS - N_EXPERTS), F32)], axis=1)
    w["wr_hi"], w["wr_lo"] = _split_bf16(wr)
    w["b_r"] = jnp.concatenate([p["b_rg"], p["b_re"],
                                jnp.zeros((ROUTE_W - N_GROUPS - N_EXPERTS,), F32)]).reshape(1, -1)
    w["w_e1"] = p["w_e1"].astype(BF16)
    w["w_e3"] = p["w_e3"].astype(BF16)
    w["w_e2"] = p["w_e2"].astype(BF16)
    return w


def _tile(n, pref):
    return pref if n % pref == 0 else n


def _run(x, mem, w):
    b, s, _ = x.shape
    t = b * s
    n_mem = mem.shape[1]
    x2d = x.reshape(t, D_MODEL)
    tm = _tile(s, 512)
    scale = QK_DIM ** -0.5 * math.log2(math.e)
    cos_t, sin_t = _rope_tables(s)
    wl = dict(w, cos_q=cos_t * scale, sin_q=sin_t * scale, cos_k=cos_t, sin_k=sin_t)

    kvc = _matmul(mem.reshape(b * n_mem, D_MODEL), w["w_ckv"], BF16, n_mem, 1024)
    kvc = kvc.reshape(b, n_mem, 2 * D_MODEL)
    kc, vc = kvc[:, :, :D_MODEL], kvc[:, :, D_MODEL:]

    q, k, v, u = _in_proj(x2d, wl, s, tm)
    hw = MLA_HEADS * HEAD_PAD
    att = _mla(q.reshape(b, s, hw), k.reshape(b, s, hw), v.reshape(b, s, hw), _tile(s, 512), s)

    nc = s // SSM_L
    u_c = (u.astype(BF16).reshape(b, nc, SSM_L, SSM_G, SSM_GC).transpose(0, 3, 1, 2, 4)
           .reshape(b, SSM_G, nc, SSM_CW))
    y_c = _s5(u_c, w)
    y = (y_c.reshape(b, SSM_G, nc, SSM_L, SSM_GC).transpose(0, 2, 3, 1, 4).reshape(t, SSM_W))

    tmx = _tile(s, 512)
    x2, gates = _mix(x2d, att.reshape(t, ATT_W), y, u, kc, vc, w, s, tmx)

    tm_e = 256
    tm_c = _tile(t, 256)
    row_src, row_gate, tile_expert, n_valid, pos_tiles = _route(gates, tm_e, tm_c)
    y_sorted = _experts(x2, row_src, row_gate, tile_expert, n_valid, w, tm_e)
    out = _combine(x2, y_sorted, pos_tiles, w, tm_c)
    return out.reshape(b, s, D_MODEL)


def kernel(x_prompt, x_sample, mem_prompt, mem_sample, w_in, g_q_lat, w_uq, g_kv_lat, w_ukv, ssm_a_re, ssm_a_im, ssm_log_dt, ssm_b_re, ssm_b_im, ssm_c_re, ssm_c_im, ssm_d, w_glu, g_attn_grp, g_ssm_grp, w_out, ln1_g, ln1_b, w_cq, w_ckv, w_co, ln2_g, ln2_b, w_rg, b_rg, w_re, b_re, w_e1, w_e3, w_e2, ln3_g, ln3_b):
    params = dict(w_in=w_in, g_q_lat=g_q_lat, w_uq=w_uq, g_kv_lat=g_kv_lat, w_ukv=w_ukv,
                  ssm_a_re=ssm_a_re, ssm_a_im=ssm_a_im, ssm_log_dt=ssm_log_dt, ssm_b_re=ssm_b_re,
                  ssm_b_im=ssm_b_im, ssm_c_re=ssm_c_re, ssm_c_im=ssm_c_im, ssm_d=ssm_d, w_glu=w_glu,
                  g_attn_grp=g_attn_grp, g_ssm_grp=g_ssm_grp, w_out=w_out, ln1_g=ln1_g, ln1_b=ln1_b,
                  w_cq=w_cq, w_ckv=w_ckv, w_co=w_co, ln2_g=ln2_g, ln2_b=ln2_b,
                  w_rg=w_rg, b_rg=b_rg, w_re=w_re, b_re=b_re, w_e1=w_e1, w_e3=w_e3, w_e2=w_e2,
                  ln3_g=ln3_g, ln3_b=ln3_b)
    params = {name: val[0] for name, val in params.items()}
    max_chunks = max(x_prompt.shape[1], x_sample.shape[1]) // SSM_L
    w = _prepare(params, max_chunks)
    return (_run(x_prompt, mem_prompt, w), _run(x_sample, mem_sample, w))
```

```python
import functools
import math

import jax
import jax.numpy as jnp
from jax import lax
from jax.experimental import pallas as pl
from jax.experimental.pallas import tpu as pltpu

F32 = jnp.float32
BF16 = jnp.bfloat16

D_MODEL = 1024
MLA_HEADS = 8
NOPE_DIM = 64
ROPE_DIM = 32
V_DIM = 64
QK_DIM = NOPE_DIM + ROPE_DIM
Q_LORA = 384
KV_LORA = 256
ATT_W = MLA_HEADS * V_DIM
ROPE_THETA = 10000.0
SSM_W = D_MODEL - ATT_W
SSM_GC = 16
SSM_G = SSM_W // SSM_GC
SSM_P = 64
XA_HEADS = 4
XA_DIM = D_MODEL // XA_HEADS
N_GROUPS = 4
EXP_PER_GROUP = 8
N_EXPERTS = N_GROUPS * EXP_PER_GROUP
TOP_K = 2
D_FF_EXPERT = 256
DEPTH = 1
ALPHA = (2 * DEPTH) ** 0.25
LN_EPS = 1e-5
RMS_EPS = 1e-6

LANE = 128
HEAD_PAD = 128
SSM_L = 16
SSM_LG = LANE // SSM_GC
ROUTE_W = 128
ROW_SUB = D_MODEL // LANE
GATHER_UNROLL = 8
VMEM_LIMIT = 56 * 1024 * 1024


def _cparams(sem):
    return pltpu.CompilerParams(dimension_semantics=sem, vmem_limit_bytes=VMEM_LIMIT)


def _rms(x, g):
    return x * lax.rsqrt(jnp.mean(x * x, axis=-1, keepdims=True) + RMS_EPS) * g


def _ln(x, g, b):
    mu = jnp.mean(x, axis=-1, keepdims=True)
    xc = x - mu
    var = jnp.mean(xc * xc, axis=-1, keepdims=True)
    return xc * lax.rsqrt(var + LN_EPS) * g + b


def _dot(a, b):
    return jnp.dot(a, b, preferred_element_type=F32)


def _dot_t(a, b):
    return lax.dot_general(a, b, (((1,), (1,)), ((), ())), preferred_element_type=F32)


def _matmul_kernel(x_ref, w_ref, o_ref):
    o_ref[...] = _dot(x_ref[...].astype(BF16), w_ref[...]).astype(o_ref.dtype)


def _matmul(x, w, out_dtype, tm, tn):
    m, k = x.shape
    n = w.shape[1]
    return pl.pallas_call(
        _matmul_kernel,
        grid=(m // tm, n // tn),
        in_specs=[pl.BlockSpec((tm, k), lambda i, j: (i, 0)),
                  pl.BlockSpec((k, tn), lambda i, j: (0, j))],
        out_specs=pl.BlockSpec((tm, tn), lambda i, j: (i, j)),
        out_shape=jax.ShapeDtypeStruct((m, n), out_dtype),
        compiler_params=_cparams(("parallel", "parallel")),
        name="mem_kv",
    )(x, w)


def _rot(blk, cos, sin):
    return blk * cos + pltpu.roll(blk, HEAD_PAD - ROPE_DIM, axis=1) * sin


def _in_proj_kernel(x_ref, w_in_ref, gq_ref, gkv_ref, w_uq_ref, w_kn_ref, w_v_ref, vone_ref,
                    cq_ref, sq_ref, ck_ref, sk_ref,
                    q_ref, k_ref, v_ref, u_ref):
    xb = x_ref[...].astype(BF16)
    proj = _dot(xb, w_in_ref[...])
    o_kv = Q_LORA
    o_u = Q_LORA + KV_LORA
    o_kr = o_u + SSM_W
    u_ref[...] = proj[:, o_u:o_kr]
    qn = _rms(proj[:, :Q_LORA], gq_ref[...]).astype(BF16)
    kvn = _rms(proj[:, o_kv:o_u], gkv_ref[...]).astype(BF16)
    qa = _dot(qn, w_uq_ref[...])
    kn = _dot(kvn, w_kn_ref[...])
    v = _dot(kvn, w_v_ref[...]) + vone_ref[...]
    v_ref[...] = v.astype(BF16)
    cq, sq, ck, sk = cq_ref[...], sq_ref[...], ck_ref[...], sk_ref[...]
    kr = _rot(proj[:, o_kr:o_kr + HEAD_PAD], ck, sk)
    for h in range(MLA_HEADS):
        sl = slice(h * HEAD_PAD, (h + 1) * HEAD_PAD)
        q_ref[:, sl] = _rot(qa[:, sl], cq, sq).astype(BF16)
        k_ref[:, sl] = (kn[:, sl] + kr).astype(BF16)


def _in_proj(x2d, w, seq, tm):
    t = x2d.shape[0]
    nseq = seq // tm
    hw = MLA_HEADS * HEAD_PAD
    const = lambda i: (0, 0)
    row = lambda i: (i, 0)
    pos = lambda i: (i % nseq, 0)
    n_in = w["w_in"].shape[1]
    return pl.pallas_call(
        _in_proj_kernel,
        grid=(t // tm,),
        in_specs=[pl.BlockSpec((tm, D_MODEL), row),
                  pl.BlockSpec((D_MODEL, n_in), const),
                  pl.BlockSpec((1, Q_LORA), const),
                  pl.BlockSpec((1, KV_LORA), const),
                  pl.BlockSpec((Q_LORA, hw), const),
                  pl.BlockSpec((KV_LORA, hw), const),
                  pl.BlockSpec((KV_LORA, hw), const),
                  pl.BlockSpec((1, hw), const),
                  pl.BlockSpec((tm, HEAD_PAD), pos),
                  pl.BlockSpec((tm, HEAD_PAD), pos),
                  pl.BlockSpec((tm, HEAD_PAD), pos),
                  pl.BlockSpec((tm, HEAD_PAD), pos)],
        out_specs=[pl.BlockSpec((tm, hw), row),
                   pl.BlockSpec((tm, hw), row),
                   pl.BlockSpec((tm, hw), row),
                   pl.BlockSpec((tm, SSM_W), row)],
        out_shape=[jax.ShapeDtypeStruct((t, hw), BF16),
                   jax.ShapeDtypeStruct((t, hw), BF16),
                   jax.ShapeDtypeStruct((t, hw), BF16),
                   jax.ShapeDtypeStruct((t, SSM_W), F32)],
        compiler_params=_cparams(("parallel",)),
        name="in_proj",
    )(x2d, w["w_in"], w["g_q"], w["g_kv"], w["w_uq"], w["w_kn"], w["w_v"], w["v_one"],
      w["cos_q"], w["sin_q"], w["cos_k"], w["sin_k"])


def _shift_rows(x, d, down):
    n = x.shape[0]
    r = lax.broadcasted_iota(jnp.int32, x.shape, 0)
    if down:
        return jnp.where(r >= d, pltpu.roll(x, d, axis=0), 0.0)
    return jnp.where(r < n - d, pltpu.roll(x, n - d, axis=0), 0.0)


def _s5_kernel(u_ref, mt_ref, bin_ref, cout_ref, pw_ref, y_ref, *, n_steps):
    nc = u_ref.shape[1] // SSM_L
    sw = SSM_LG * SSM_P
    lhs = jnp.concatenate([u_ref[0, pl.ds(s, nc, stride=SSM_L), :].astype(BF16) for s in range(SSM_L)],
                          axis=1)
    hin = _dot(lhs, bin_ref[0])
    f_re, f_im, b_re, b_im = (hin[:, q * sw:(q + 1) * sw] for q in range(4))
    for s in range(n_steps):
        d = 1 << s
        ar, ai = pw_ref[0, s, 0:1, :], pw_ref[0, s, 1:2, :]
        sr, si = _shift_rows(f_re, d, True), _shift_rows(f_im, d, True)
        f_re, f_im = f_re + ar * sr - ai * si, f_im + ar * si + ai * sr
        ar, ai = pw_ref[0, s, 2:3, :], pw_ref[0, s, 3:4, :]
        sr, si = _shift_rows(b_re, d, False), _shift_rows(b_im, d, False)
        b_re, b_im = b_re + ar * sr - ai * si, b_im + ar * si + ai * sr
    hprev = jnp.concatenate([_shift_rows(f_re, 1, True), _shift_rows(f_im, 1, True),
                             _shift_rows(b_re, 1, False), _shift_rows(b_im, 1, False)], axis=1)
    y = _dot(lhs, mt_ref[0]) + _dot(hprev.astype(BF16), cout_ref[0])
    for i in range(SSM_L):
        y_ref[0, pl.ds(i, nc, stride=SSM_L), :] = y[:, i * LANE:(i + 1) * LANE]


def _s5(u, w):
    b, s, _ = u.shape
    nc = s // SSM_L
    n_steps = max(1, int(math.log2(nc)))
    assert (1 << n_steps) == nc
    pw = w["ssm_pw"][:, :n_steps]
    n_blk = SSM_W // LANE
    cw = SSM_L * LANE
    once = pl.Buffered(1)
    return pl.pallas_call(
        functools.partial(_s5_kernel, n_steps=n_steps),
        grid=(n_blk, b),
        in_specs=[pl.BlockSpec((1, s, LANE), lambda gi, bi: (bi, 0, gi)),
                  pl.BlockSpec((1, cw, cw), lambda gi, bi: (gi, 0, 0), pipeline_mode=once),
                  pl.BlockSpec((1, cw, 4 * SSM_LG * SSM_P), lambda gi, bi: (gi, 0, 0), pipeline_mode=once),
                  pl.BlockSpec((1, 4 * SSM_LG * SSM_P, cw), lambda gi, bi: (gi, 0, 0), pipeline_mode=once),
                  pl.BlockSpec((1, n_steps, 4, SSM_LG * SSM_P), lambda gi, bi: (gi, 0, 0, 0))],
        out_specs=pl.BlockSpec((1, s, LANE), lambda gi, bi: (bi, 0, gi)),
        out_shape=jax.ShapeDtypeStruct((b, s, SSM_W), F32),
        compiler_params=_cparams(("arbitrary", "arbitrary")),
        name="s5",
    )(u, w["ssm_mt"], w["ssm_bin"], w["ssm_cout"], pw)


def _mla_kernel(q_ref, k_ref, v_ref, o_ref, *, tk):
    s_len = k_ref.shape[1]
    tq = q_ref.shape[1]
    qs = [q_ref[0, :, hh * HEAD_PAD:(hh + 1) * HEAD_PAD] for hh in range(2)]

    def body(j, carry):
        off = pl.multiple_of(j * tk, tk)
        new = []
        for hh in range(2):
            m, acc = carry[hh]
            sl = slice(hh * HEAD_PAD, (hh + 1) * HEAD_PAD)
            s = _dot_t(qs[hh], k_ref[0, pl.ds(off, tk), sl])
            m_new = jnp.maximum(m, jnp.max(s, axis=-1, keepdims=True))
            p = jnp.exp2(s - m_new)
            acc = jnp.exp2(m - m_new) * acc + _dot(p.astype(BF16), v_ref[0, pl.ds(off, tk), sl])
            new.append((m_new, acc))
        return tuple(new)

    init = tuple((jnp.full((tq, 1), -jnp.inf, F32), jnp.zeros((tq, HEAD_PAD), F32)) for _ in range(2))
    res = lax.fori_loop(0, s_len // tk, body, init, unroll=True)
    outs = [acc / acc[:, V_DIM:V_DIM + 1] for _, acc in res]
    lane = lax.broadcasted_iota(jnp.int32, (tq, HEAD_PAD), 1)
    o = jnp.where(lane < V_DIM, outs[0], pltpu.roll(outs[1], V_DIM, axis=1))
    o_ref[0] = o.astype(o_ref.dtype)


def _mla(q, k, v, tq, tk):
    b, s, hw = q.shape
    return pl.pallas_call(
        functools.partial(_mla_kernel, tk=tk),
        grid=(b, MLA_HEADS // 2, s // tq),
        in_specs=[pl.BlockSpec((1, tq, 2 * HEAD_PAD), lambda bi, hi, qi: (bi, qi, hi)),
                  pl.BlockSpec((1, s, 2 * HEAD_PAD), lambda bi, hi, qi: (bi, 0, hi)),
                  pl.BlockSpec((1, s, 2 * HEAD_PAD), lambda bi, hi, qi: (bi, 0, hi))],
        out_specs=pl.BlockSpec((1, tq, 2 * V_DIM), lambda bi, hi, qi: (bi, qi, hi)),
        out_shape=jax.ShapeDtypeStruct((b, s, ATT_W), BF16),
        compiler_params=_cparams(("parallel", "parallel", "arbitrary")),
        name="mla",
    )(q, k, v)


def _split_bf16(x):
    hi = x.astype(BF16)
    lo = (x - hi.astype(F32)).astype(BF16)
    return hi, lo


def _mix_kernel(x_ref, att_ref, y_ref, u_ref, d_ref, w_glu_ref, ga_ref, gs_ref, wo_a_ref, wo_s_ref,
                ln1g_ref, ln1b_ref, w_cq_ref, kc_ref, vc_ref, w_co_ref, ln2g_ref, ln2b_ref,
                wr_hi_ref, wr_lo_ref, br_ref, x2_ref, lg_ref):
    y = y_ref[...] + d_ref[...] * u_ref[...]
    gy = jax.nn.gelu(y)
    ssm = gy * jax.nn.sigmoid(_dot(gy.astype(BF16), w_glu_ref[...]))
    att = att_ref[...].astype(F32)
    mixed = (_dot(_rms(att, ga_ref[...]).astype(BF16), wo_a_ref[...])
             + _dot(_rms(ssm, gs_ref[...]).astype(BF16), wo_s_ref[...]))
    x1 = _ln(ALPHA * x_ref[...] + mixed, ln1g_ref[...], ln1b_ref[...])
    qc = (_dot(x1.astype(BF16), w_cq_ref[...]) * (XA_DIM ** -0.5)).astype(BF16)
    heads = []
    for h in range(XA_HEADS):
        sl = slice(h * XA_DIM, (h + 1) * XA_DIM)
        s = _dot_t(qc[:, sl], kc_ref[0, :, sl])
        p = jnp.exp(s - jnp.max(s, axis=-1, keepdims=True))
        o = _dot(p.astype(BF16), vc_ref[0, :, sl])
        heads.append(o / jnp.sum(p, axis=-1, keepdims=True))
    ca = _dot(jnp.concatenate(heads, axis=1).astype(BF16), w_co_ref[...])
    x2 = _ln(ALPHA * x1 + ca, ln2g_ref[...], ln2b_ref[...])
    _store_rows(x2_ref, x2)
    hi, lo = _split_bf16(x2)
    logits = (_dot(hi, wr_hi_ref[...]) + _dot(lo, wr_hi_ref[...]) + _dot(hi, wr_lo_ref[...])
              + br_ref[...])
    lg_ref[...] = _gates(logits)


def _first_max(x, lane):
    v = jnp.max(x, axis=-1, keepdims=True)
    i = jnp.min(jnp.where(x == v, lane, ROUTE_W), axis=-1, keepdims=True)
    return v, i


def _gates(logits):
    lane = lax.broadcasted_iota(jnp.int32, logits.shape, 1)
    is_grp = lane < N_GROUPS
    lg = jnp.where(is_grp, logits, -jnp.inf)
    g_max, grp = _first_max(lg, lane)
    p_grp = 1.0 / jnp.sum(jnp.where(is_grp, jnp.exp(lg - g_max), 0.0), axis=-1, keepdims=True)
    e_lane = lane - N_GROUPS
    in_grp = (e_lane >= 0) & (e_lane < N_EXPERTS) & ((e_lane // EXP_PER_GROUP) == grp)
    le = jnp.where(in_grp, logits, -jnp.inf)
    v1, i1 = _first_max(le, lane)
    v2, i2 = _first_max(jnp.where(lane == i1, -jnp.inf, le), lane)
    e21 = jnp.exp(v2 - v1)
    w1 = p_grp / (1.0 + e21)
    w2 = w1 * e21
    out = jnp.where(lane == 0, (i1 - N_GROUPS).astype(F32),
                    jnp.where(lane == 1, (i2 - N_GROUPS).astype(F32),
                              jnp.where(lane == 2, w1, jnp.where(lane == 3, w2, 0.0))))
    return out


def _mix(x2d, att, y, u, kvc, w, seq, tm):
    t = x2d.shape[0]
    nseq = seq // tm
    n_mem = kvc.shape[1]
    const = lambda i: (0, 0)
    row = lambda i: (i, 0)
    mem_k = lambda i: (i // nseq, 0, 0)
    mem_v = lambda i: (i // nseq, 0, 1)
    vec = lambda n: pl.BlockSpec((1, n), const)
    return pl.pallas_call(
        _mix_kernel,
        grid=(t // tm,),
        in_specs=[pl.BlockSpec((tm, D_MODEL), row),
                  pl.BlockSpec((tm, ATT_W), row),
                  pl.BlockSpec((tm, SSM_W), row),
                  pl.BlockSpec((tm, SSM_W), row),
                  vec(SSM_W),
                  pl.BlockSpec((SSM_W, SSM_W), const),
                  vec(ATT_W), vec(SSM_W),
                  pl.BlockSpec((ATT_W, D_MODEL), const),
                  pl.BlockSpec((SSM_W, D_MODEL), const),
                  vec(D_MODEL), vec(D_MODEL),
                  pl.BlockSpec((D_MODEL, D_MODEL), const),
                  pl.BlockSpec((1, n_mem, D_MODEL), mem_k),
                  pl.BlockSpec((1, n_mem, D_MODEL), mem_v),
                  pl.BlockSpec((D_MODEL, D_MODEL), const),
                  vec(D_MODEL), vec(D_MODEL),
                  pl.BlockSpec((D_MODEL, ROUTE_W), const),
                  pl.BlockSpec((D_MODEL, ROUTE_W), const),
                  vec(ROUTE_W)],
        out_specs=[pl.BlockSpec((tm * ROW_SUB, LANE), row),
                   pl.BlockSpec((tm, ROUTE_W), row)],
        out_shape=[jax.ShapeDtypeStruct((t * ROW_SUB, LANE), F32),
                   jax.ShapeDtypeStruct((t, ROUTE_W), F32)],
        compiler_params=_cparams(("parallel",)),
        name="mix",
    )(x2d, att, y, u, w["ssm_d"], w["w_glu"], w["g_attn"], w["g_ssm"], w["w_out_a"], w["w_out_s"],
      w["ln1_g"], w["ln1_b"], w["w_cq"], kvc, kvc, w["w_co"], w["ln2_g"], w["ln2_b"],
      w["wr_hi"], w["wr_lo"], w["b_r"])


def _row_copy(src_hbm, dst_vmem, sem, src_row, dst_row):
    return pltpu.make_async_copy(src_hbm.at[pl.ds(pl.multiple_of(src_row * ROW_SUB, ROW_SUB), ROW_SUB)],
                                 dst_vmem.at[pl.ds(pl.multiple_of(dst_row * ROW_SUB, ROW_SUB), ROW_SUB)], sem)


def _load_rows(ref, n_rows):
    return jnp.concatenate([ref[pl.ds(j, n_rows, stride=ROW_SUB), :] for j in range(ROW_SUB)], axis=1)


def _store_rows(ref, val):
    n_rows = val.shape[0]
    for j in range(ROW_SUB):
        ref[pl.ds(j, n_rows, stride=ROW_SUB), :] = val[:, j * LANE:(j + 1) * LANE]


def _gather_rows(src_hbm, idx_smem, dst_vmem, sem, n_rows):
    def issue(r, c):
        _row_copy(src_hbm, dst_vmem, sem, idx_smem[r], r).start()
        return c

    lax.fori_loop(0, n_rows, issue, 0, unroll=GATHER_UNROLL)


def _wait_rows(src_hbm, dst_vmem, sem, n_rows):
    def drain(r, c):
        _row_copy(src_hbm, dst_vmem, sem, 0, r).wait()
        return c

    lax.fori_loop(0, n_rows, drain, 0, unroll=GATHER_UNROLL)


def _fetch_tile(i, idx_hbm, idx_smem, isem, src_hbm, buf, gsem, n_rows):
    cp = pltpu.make_async_copy(idx_hbm.at[i], idx_smem, isem)
    cp.start()
    cp.wait()
    _gather_rows(src_hbm, idx_smem, buf, gsem, n_rows)
    _wait_rows(src_hbm, buf, gsem, n_rows)


def _experts_kernel(te_ref, nv_ref, src_hbm, x_hbm, gate_ref, w1_ref, w3_ref, w2_ref, y_ref,
                    idx_smem, xbuf, isem, gsem):
    i = pl.program_id(0)
    n = nv_ref[0]
    tm = xbuf.shape[0] // ROW_SUB

    @pl.when(i < n)
    def _():
        _fetch_tile(i, src_hbm, idx_smem, isem, x_hbm, xbuf, gsem, tm)
        xb = _load_rows(xbuf, tm).astype(BF16)
        hdn = jax.nn.silu(_dot(xb, w1_ref[0])) * _dot(xb, w3_ref[0])
        _store_rows(y_ref, gate_ref[...] * _dot(hdn.astype(BF16), w2_ref[0]))

    @pl.when(i >= n)
    def _():
        y_ref[...] = jnp.zeros(y_ref.shape, y_ref.dtype)


def _experts(x2, row_src, row_gate, tile_expert, n_valid, w, tm):
    n_tiles = row_src.shape[0]
    ew = lambda i, te, nv: (te[i], 0, 0)
    grid_spec = pltpu.PrefetchScalarGridSpec(
        num_scalar_prefetch=2,
        grid=(n_tiles,),
        in_specs=[pl.BlockSpec(memory_space=pl.ANY),
                  pl.BlockSpec(memory_space=pl.ANY),
                  pl.BlockSpec((tm, 1), lambda i, te, nv: (i, 0)),
                  pl.BlockSpec((1, D_MODEL, D_FF_EXPERT), ew),
                  pl.BlockSpec((1, D_MODEL, D_FF_EXPERT), ew),
                  pl.BlockSpec((1, D_FF_EXPERT, D_MODEL), ew)],
        out_specs=pl.BlockSpec((tm * ROW_SUB, LANE), lambda i, te, nv: (i, 0)),
        scratch_shapes=[pltpu.SMEM((tm,), jnp.int32),
                        pltpu.VMEM((tm * ROW_SUB, LANE), F32),
                        pltpu.SemaphoreType.DMA,
                        pltpu.SemaphoreType.DMA],
    )
    return pl.pallas_call(
        _experts_kernel,
        grid_spec=grid_spec,
        out_shape=jax.ShapeDtypeStruct((n_tiles * tm * ROW_SUB, LANE), F32),
        compiler_params=_cparams(("arbitrary",)),
        name="experts",
    )(tile_expert, n_valid, row_src, x2, row_gate, w["w_e1"], w["w_e3"], w["w_e2"])


def _combine_kernel(pos_hbm, x2_ref, y_hbm, g_ref, b_ref, o_ref, idx_smem, ybuf, isem, gsem):
    i = pl.program_id(0)
    tm = o_ref.shape[0]
    _fetch_tile(i, pos_hbm, idx_smem, isem, y_hbm, ybuf, gsem, TOP_K * tm)
    rows = _load_rows(ybuf, TOP_K * tm)
    moe = rows[0:tm] + rows[tm:2 * tm]
    o_ref[...] = _ln(ALPHA * _load_rows(x2_ref, tm) + moe, g_ref[...], b_ref[...])


def _combine(x2, y_sorted, pos_tiles, w, tm):
    t = x2.shape[0] // ROW_SUB
    const = lambda i: (0, 0)
    row = lambda i: (i, 0)
    return pl.pallas_call(
        _combine_kernel,
        grid=(t // tm,),
        in_specs=[pl.BlockSpec(memory_space=pl.ANY),
                  pl.BlockSpec((tm * ROW_SUB, LANE), row),
                  pl.BlockSpec(memory_space=pl.ANY),
                  pl.BlockSpec((1, D_MODEL), const),
                  pl.BlockSpec((1, D_MODEL), const)],
        out_specs=pl.BlockSpec((tm, D_MODEL), row),
        out_shape=jax.ShapeDtypeStruct((t, D_MODEL), F32),
        scratch_shapes=[pltpu.SMEM((TOP_K * tm,), jnp.int32),
                        pltpu.VMEM((TOP_K * tm * ROW_SUB, LANE), F32),
                        pltpu.SemaphoreType.DMA,
                        pltpu.SemaphoreType.DMA],
        compiler_params=_cparams(("arbitrary",)),
        name="combine",
    )(pos_tiles, x2, y_sorted, w["ln3_g"], w["ln3_b"])


def _route(gates, tm_e, tm_c):
    t = gates.shape[0]
    flat_e = gates[:, :TOP_K].astype(jnp.int32).reshape(-1)
    flat_w = gates[:, TOP_K:2 * TOP_K].reshape(-1)
    n = flat_e.shape[0]
    ids = jnp.arange(n, dtype=jnp.int32)
    sorted_e, order = lax.sort((flat_e, ids), num_keys=1, is_stable=True)
    _, inv = lax.sort((order, ids), num_keys=1)
    start_unp = jnp.sum(sorted_e[None, :] < jnp.arange(N_EXPERTS + 1, dtype=jnp.int32)[:, None], axis=1).astype(jnp.int32)
    counts = start_unp[1:] - start_unp[:-1]
    start_unp = start_unp[:-1]
    padded = ((counts + tm_e - 1) // tm_e) * tm_e
    start_pad = jnp.cumsum(padded) - padded
    pos = (start_pad[flat_e] + inv - start_unp[flat_e]).astype(jnp.int32)

    n_tiles = n // tm_e + N_EXPERTS
    rows = jnp.arange(n_tiles * tm_e, dtype=jnp.int32)
    end_pad = start_pad + padded
    row_e = jnp.minimum(jnp.sum(end_pad[None, :] <= rows[:, None], axis=1), N_EXPERTS - 1).astype(jnp.int32)
    within = rows - start_pad[row_e]
    valid = within < counts[row_e]
    src = order[jnp.clip(start_unp[row_e] + within, 0, n - 1)]
    row_src = jnp.where(valid, src // TOP_K, 0).astype(jnp.int32)
    row_gate = jnp.where(valid, flat_w[src], 0.0).astype(F32)
    tile_expert = row_e[::tm_e]
    n_valid = (jnp.sum(padded) // tm_e).astype(jnp.int32).reshape(1)
    pos_tiles = pos.reshape(t // tm_c, tm_c, TOP_K).transpose(0, 2, 1).reshape(t // tm_c, TOP_K * tm_c)
    return (row_src.reshape(n_tiles, tm_e), row_gate.reshape(n_tiles * tm_e, 1), tile_expert, n_valid,
            pos_tiles)


def _rope_tables(seq):
    inv = ROPE_THETA ** (-jnp.arange(0, ROPE_DIM, 2, dtype=F32) / ROPE_DIM)
    ang = jnp.arange(seq, dtype=F32)[:, None] * inv[None, :]
    c, s = jnp.cos(ang), jnp.sin(ang)
    z = jnp.zeros((seq, HEAD_PAD - NOPE_DIM - ROPE_DIM), F32)
    cos_t = jnp.concatenate([jnp.ones((seq, NOPE_DIM), F32), c, c, z], axis=1)
    sin_t = jnp.concatenate([jnp.zeros((seq, NOPE_DIM), F32), s, s, z], axis=1)
    return cos_t, sin_t


def _pair_rot(w_rope):
    half = ROPE_DIM // 2
    return jnp.concatenate([-w_rope[..., half:], w_rope[..., :half]], axis=-1)


def _ssm_operators(a_re, a_im, log_dt, b_re, b_im, c_re, c_im, max_steps):
    a = lax.complex(a_re, a_im)
    dt = jnp.exp(log_dt)[..., None]
    adt = a * dt
    a_bar = jnp.exp(adt)
    b_bar = ((a_bar - 1.0) / a)[..., None] * lax.complex(b_re, b_im)
    cm = lax.complex(c_re, c_im)
    taus = jnp.arange(SSM_L + 1, dtype=F32)
    pw = jnp.exp(adt[None] * taus[:, None, None, None])
    kf = jnp.real(jnp.einsum("gcp,tgp,gpd->tgcd", cm[0], pw[:SSM_L, 0], b_bar[0]))
    kb = jnp.real(jnp.einsum("gcp,tgp,gpd->tgcd", cm[1], pw[:SSM_L, 1], b_bar[1]))
    kfull = jnp.concatenate([kb[:0:-1], kf[:1] + kb[:1], kf[1:]], axis=0)
    ii = jnp.arange(SSM_L)
    lag = ii[:, None] - ii[None, :] + SSM_L - 1
    mt = kfull[lag]
    mt = mt.transpose(2, 1, 4, 0, 3)
    pf = pw[SSM_L - 1 - ii, 0]
    pb = pw[ii, 1]
    sf = pf[..., None] * b_bar[0][None]
    sb = pb[..., None] * b_bar[1][None]
    bin_ = jnp.stack([jnp.real(sf), jnp.imag(sf), jnp.real(sb), jnp.imag(sb)], axis=2)
    bin_ = bin_.transpose(1, 0, 4, 2, 3)
    of = cm[0][None] * pw[1 + ii, 0][:, :, None, :]
    ob = cm[1][None] * pw[SSM_L - ii, 1][:, :, None, :]
    cout = jnp.stack([jnp.real(of), -jnp.imag(of), jnp.real(ob), -jnp.imag(ob)], axis=3)
    cout = cout.transpose(1, 3, 4, 0, 2)
    n_blk = SSM_G // SSM_LG
    eye = jnp.eye(SSM_LG, dtype=F32)
    blk = lambda t: t.reshape((n_blk, SSM_LG) + t.shape[1:])
    cw = SSM_L * LANE
    sw4 = 4 * SSM_LG * SSM_P
    mt_b = jnp.einsum("gh,ngjdic->njgdihc", eye, blk(mt)).reshape(n_blk, cw, cw)
    bin_b = jnp.einsum("gh,ngjdqp->njgdqhp", eye, blk(bin_)).reshape(n_blk, cw, sw4)
    cout_b = jnp.einsum("gh,ngqpic->nqgpihc", eye, blk(cout)).reshape(n_blk, sw4, cw)
    steps = (SSM_L * (2.0 ** jnp.arange(max_steps, dtype=F32)))
    sp = jnp.exp(adt[None] * steps[:, None, None, None])
    tab = jnp.stack([jnp.real(sp[:, 0]), jnp.imag(sp[:, 0]), jnp.real(sp[:, 1]), jnp.imag(sp[:, 1])],
                    axis=1)
    tab = tab.reshape(max_steps, 4, n_blk, SSM_LG * SSM_P).transpose(2, 0, 1, 3)
    return mt_b.astype(BF16), bin_b.astype(BF16), cout_b.astype(BF16), tab.astype(F32)


def _prepare(p, max_chunks):
    w = {}
    w_in = p["w_in"]
    o_kpe = Q_LORA + KV_LORA
    kpe = w_in[:, o_kpe:o_kpe + ROPE_DIM]
    kpe_blk = jnp.concatenate([jnp.zeros((D_MODEL, NOPE_DIM), F32), kpe, _pair_rot(kpe)], axis=1)
    w["w_in"] = jnp.concatenate([w_in[:, :o_kpe], w_in[:, o_kpe + ROPE_DIM:], kpe_blk], axis=1).astype(BF16)
    w["g_q"] = p["g_q_lat"].reshape(1, -1)
    w["g_kv"] = p["g_kv_lat"].reshape(1, -1)
    uq = p["w_uq"].reshape(Q_LORA, MLA_HEADS, QK_DIM)
    rope = uq[..., NOPE_DIM:]
    w["w_uq"] = jnp.concatenate([uq, _pair_rot(rope)], axis=-1).reshape(Q_LORA, -1).astype(BF16)
    ukv = p["w_ukv"].reshape(KV_LORA, MLA_HEADS, NOPE_DIM + V_DIM)
    zpad = jnp.zeros((KV_LORA, MLA_HEADS, HEAD_PAD - NOPE_DIM), F32)
    w["w_kn"] = jnp.concatenate([ukv[..., :NOPE_DIM], zpad], axis=-1).reshape(KV_LORA, -1).astype(BF16)
    w["w_v"] = jnp.concatenate([ukv[..., NOPE_DIM:], zpad], axis=-1).reshape(KV_LORA, -1).astype(BF16)
    one = jnp.zeros((MLA_HEADS, HEAD_PAD), F32).at[:, V_DIM].set(1.0)
    w["v_one"] = one.reshape(1, -1)
    max_steps = max(1, int(math.log2(max_chunks)))
    w["ssm_mt"], w["ssm_bin"], w["ssm_cout"], w["ssm_pw"] = _ssm_operators(
        p["ssm_a_re"], p["ssm_a_im"], p["ssm_log_dt"], p["ssm_b_re"], p["ssm_b_im"],
        p["ssm_c_re"], p["ssm_c_im"], max_steps)
    w["ssm_d"] = p["ssm_d"].reshape(1, -1)
    w["w_glu"] = p["w_glu"].astype(BF16)
    w["g_attn"] = p["g_attn_grp"].reshape(1, -1)
    w["g_ssm"] = p["g_ssm_grp"].reshape(1, -1)
    w["w_out_a"] = p["w_out"][:ATT_W].astype(BF16)
    w["w_out_s"] = p["w_out"][ATT_W:].astype(BF16)
    for n in ("ln1_g", "ln1_b", "ln2_g", "ln2_b", "ln3_g", "ln3_b"):
        w[n] = p[n].reshape(1, -1)
    w["w_cq"] = p["w_cq"].astype(BF16)
    w["w_ckv"] = p["w_ckv"].astype(BF16)
    w["w_co"] = p["w_co"].astype(BF16)
    wr = jnp.concatenate([p["w_rg"], p["w_re"],
                          jnp.zeros((D_MODEL, ROUTE_W - N_GROUPS - N_EXPERTS), F32)], axis=1)
    w["wr_hi"], w["wr_lo"] = _split_bf16(wr)
    w["b_r"] = jnp.concatenate([p["b_rg"], p["b_re"],
                                jnp.zeros((ROUTE_W - N_GROUPS - N_EXPERTS,), F32)]).reshape(1, -1)
    w["w_e1"] = p["w_e1"].astype(BF16)
    w["w_e3"] = p["w_e3"].astype(BF16)
    w["w_e2"] = p["w_e2"].astype(BF16)
    return w


def _tile(n, pref):
    return pref if n % pref == 0 else n


def _run(x, mem, w):
    b, s, _ = x.shape
    t = b * s
    n_mem = mem.shape[1]
    x2d = x.reshape(t, D_MODEL)
    tm = _tile(s, 512)
    scale = QK_DIM ** -0.5 * math.log2(math.e)
    cos_t, sin_t = _rope_tables(s)
    wl = dict(w, cos_q=cos_t * scale, sin_q=sin_t * scale, cos_k=cos_t, sin_k=sin_t)

    kvc = _matmul(mem.reshape(b * n_mem, D_MODEL), w["w_ckv"], BF16, n_mem, 1024)
    kvc = kvc.reshape(b, n_mem, 2 * D_MODEL)

    q, k, v, u = _in_proj(x2d, wl, s, tm)
    hw = MLA_HEADS * HEAD_PAD
    att = _mla(q.reshape(b, s, hw), k.reshape(b, s, hw), v.reshape(b, s, hw), _tile(s, 512), s)

    y = _s5(u.reshape(b, s, SSM_W), w).reshape(t, SSM_W)

    tmx = _tile(s, 512)
    x2, gates = _mix(x2d, att.reshape(t, ATT_W), y, u, kvc, w, s, tmx)

    tm_e = 512
    tm_c = _tile(t, 512)
    row_src, row_gate, tile_expert, n_valid, pos_tiles = _route(gates, tm_e, tm_c)
    y_sorted = _experts(x2, row_src, row_gate, tile_expert, n_valid, w, tm_e)
    out = _combine(x2, y_sorted, pos_tiles, w, tm_c)
    return out.reshape(b, s, D_MODEL)


def kernel(x_prompt, x_sample, mem_prompt, mem_sample, w_in, g_q_lat, w_uq, g_kv_lat, w_ukv, ssm_a_re, ssm_a_im, ssm_log_dt, ssm_b_re, ssm_b_im, ssm_c_re, ssm_c_im, ssm_d, w_glu, g_attn_grp, g_ssm_grp, w_out, ln1_g, ln1_b, w_cq, w_ckv, w_co, ln2_g, ln2_b, w_rg, b_rg, w_re, b_re, w_e1, w_e3, w_e2, ln3_g, ln3_b):
    params = dict(w_in=w_in, g_q_lat=g_q_lat, w_uq=w_uq, g_kv_lat=g_kv_lat, w_ukv=w_ukv,
                  ssm_a_re=ssm_a_re, ssm_a_im=ssm_a_im, ssm_log_dt=ssm_log_dt, ssm_b_re=ssm_b_re,
                  ssm_b_im=ssm_b_im, ssm_c_re=ssm_c_re, ssm_c_im=ssm_c_im, ssm_d=ssm_d, w_glu=w_glu,
                  g_attn_grp=g_attn_grp, g_ssm_grp=g_ssm_grp, w_out=w_out, ln1_g=ln1_g, ln1_b=ln1_b,
                  w_cq=w_cq, w_ckv=w_ckv, w_co=w_co, ln2_g=ln2_g, ln2_b=ln2_b,
                  w_rg=w_rg, b_rg=b_rg, w_re=w_re, b_re=b_re, w_e1=w_e1, w_e3=w_e3, w_e2=w_e2,
                  ln3_g=ln3_g, ln3_b=ln3_b)
    params = {name: val[0] for name, val in params.items()}
    max_chunks = max(x_prompt.shape[1], x_sample.shape[1]) // SSM_L
    w = _prepare(params, max_chunks)
    return (_run(x_prompt, mem_prompt, w), _run(x_sample, mem_sample, w))
```

```python
import functools
import math

import jax
import jax.numpy as jnp
from jax import lax
from jax.experimental import pallas as pl
from jax.experimental.pallas import tpu as pltpu

F32 = jnp.float32
BF16 = jnp.bfloat16

D_MODEL = 1024
MLA_HEADS = 8
NOPE_DIM = 64
ROPE_DIM = 32
V_DIM = 64
QK_DIM = NOPE_DIM + ROPE_DIM
Q_LORA = 384
KV_LORA = 256
ATT_W = MLA_HEADS * V_DIM
ROPE_THETA = 10000.0
SSM_W = D_MODEL - ATT_W
SSM_GC = 16
SSM_G = SSM_W // SSM_GC
SSM_P = 64
XA_HEADS = 4
XA_DIM = D_MODEL // XA_HEADS
N_GROUPS = 4
EXP_PER_GROUP = 8
N_EXPERTS = N_GROUPS * EXP_PER_GROUP
TOP_K = 2
D_FF_EXPERT = 256
DEPTH = 1
ALPHA = (2 * DEPTH) ** 0.25
LN_EPS = 1e-5
RMS_EPS = 1e-6

LANE = 128
HEAD_PAD = 128
SSM_L = 16
SSM_LG = LANE // SSM_GC
ROUTE_W = 128
ROW_SUB = D_MODEL // LANE
GATHER_UNROLL = 8
TOKEN_TILE = 512
EXPERT_TILE = 512
VMEM_LIMIT = 56 * 1024 * 1024


def _cparams(sem):
    return pltpu.CompilerParams(dimension_semantics=sem, vmem_limit_bytes=VMEM_LIMIT)


def _rms(x, g):
    return x * lax.rsqrt(jnp.mean(x * x, axis=-1, keepdims=True) + RMS_EPS) * g


def _ln(x, g, b):
    mu = jnp.mean(x, axis=-1, keepdims=True)
    xc = x - mu
    var = jnp.mean(xc * xc, axis=-1, keepdims=True)
    return xc * lax.rsqrt(var + LN_EPS) * g + b


def _dot(a, b):
    return jnp.dot(a, b, preferred_element_type=F32)


def _dot_t(a, b):
    return lax.dot_general(a, b, (((1,), (1,)), ((), ())), preferred_element_type=F32)


def _matmul_kernel(x_ref, w_ref, o_ref):
    o_ref[...] = _dot(x_ref[...].astype(BF16), w_ref[...]).astype(o_ref.dtype)


def _matmul(x, w, out_dtype, tm, tn):
    m, k = x.shape
    n = w.shape[1]
    return pl.pallas_call(
        _matmul_kernel,
        grid=(m // tm, n // tn),
        in_specs=[pl.BlockSpec((tm, k), lambda i, j: (i, 0)),
                  pl.BlockSpec((k, tn), lambda i, j: (0, j))],
        out_specs=pl.BlockSpec((tm, tn), lambda i, j: (i, j)),
        out_shape=jax.ShapeDtypeStruct((m, n), out_dtype),
        compiler_params=_cparams(("parallel", "parallel")),
        name="mem_kv",
    )(x, w)


def _rot(blk, cos, sin):
    return blk * cos + pltpu.roll(blk, HEAD_PAD - ROPE_DIM, axis=1) * sin


def _in_proj_kernel(x_ref, w_in_ref, gq_ref, gkv_ref, w_uq_ref, w_kn_ref, w_v_ref, vone_ref,
                    cq_ref, sq_ref, ck_ref, sk_ref,
                    q_ref, k_ref, v_ref, u_ref):
    xb = x_ref[...].astype(BF16)
    proj = _dot(xb, w_in_ref[...])
    o_kv = Q_LORA
    o_u = Q_LORA + KV_LORA
    o_kr = o_u + SSM_W
    u_ref[...] = proj[:, o_u:o_kr]
    qn = _rms(proj[:, :Q_LORA], gq_ref[...]).astype(BF16)
    kvn = _rms(proj[:, o_kv:o_u], gkv_ref[...]).astype(BF16)
    qa = _dot(qn, w_uq_ref[...])
    kn = _dot(kvn, w_kn_ref[...])
    v = _dot(kvn, w_v_ref[...]) + vone_ref[...]
    v_ref[...] = v.astype(BF16)
    cq, sq, ck, sk = cq_ref[...], sq_ref[...], ck_ref[...], sk_ref[...]
    kr = _rot(proj[:, o_kr:o_kr + HEAD_PAD], ck, sk)
    for h in range(MLA_HEADS):
        sl = slice(h * HEAD_PAD, (h + 1) * HEAD_PAD)
        q_ref[:, sl] = _rot(qa[:, sl], cq, sq).astype(BF16)
        k_ref[:, sl] = (kn[:, sl] + kr).astype(BF16)


def _in_proj(x2d, w, seq, tm):
    t = x2d.shape[0]
    nseq = seq // tm
    hw = MLA_HEADS * HEAD_PAD
    const = lambda i: (0, 0)
    row = lambda i: (i, 0)
    pos = lambda i: (i % nseq, 0)
    n_in = w["w_in"].shape[1]
    return pl.pallas_call(
        _in_proj_kernel,
        grid=(t // tm,),
        in_specs=[pl.BlockSpec((tm, D_MODEL), row),
                  pl.BlockSpec((D_MODEL, n_in), const),
                  pl.BlockSpec((1, Q_LORA), const),
                  pl.BlockSpec((1, KV_LORA), const),
                  pl.BlockSpec((Q_LORA, hw), const),
                  pl.BlockSpec((KV_LORA, hw), const),
                  pl.BlockSpec((KV_LORA, hw), const),
                  pl.BlockSpec((1, hw), const),
                  pl.BlockSpec((tm, HEAD_PAD), pos),
                  pl.BlockSpec((tm, HEAD_PAD), pos),
                  pl.BlockSpec((tm, HEAD_PAD), pos),
                  pl.BlockSpec((tm, HEAD_PAD), pos)],
        out_specs=[pl.BlockSpec((tm, hw), row),
                   pl.BlockSpec((tm, hw), row),
                   pl.BlockSpec((tm, hw), row),
                   pl.BlockSpec((tm, SSM_W), row)],
        out_shape=[jax.ShapeDtypeStruct((t, hw), BF16),
                   jax.ShapeDtypeStruct((t, hw), BF16),
                   jax.ShapeDtypeStruct((t, hw), BF16),
                   jax.ShapeDtypeStruct((t, SSM_W), F32)],
        compiler_params=_cparams(("parallel",)),
        name="in_proj",
    )(x2d, w["w_in"], w["g_q"], w["g_kv"], w["w_uq"], w["w_kn"], w["w_v"], w["v_one"],
      w["cos_q"], w["sin_q"], w["cos_k"], w["sin_k"])


def _shift_rows(x, d, down):
    n = x.shape[0]
    r = lax.broadcasted_iota(jnp.int32, x.shape, 0)
    if down:
        return jnp.where(r >= d, pltpu.roll(x, d, axis=0), 0.0)
    return jnp.where(r < n - d, pltpu.roll(x, n - d, axis=0), 0.0)


def _s5_kernel(u_ref, mt_ref, bin_ref, cout_ref, pw_ref, y_ref, *, n_steps):
    nc = u_ref.shape[1] // SSM_L
    sw = SSM_LG * SSM_P
    lhs = jnp.concatenate([u_ref[0, pl.ds(s, nc, stride=SSM_L), :].astype(BF16) for s in range(SSM_L)],
                          axis=1)
    hin = _dot(lhs, bin_ref[0])
    f_re, f_im, b_re, b_im = (hin[:, q * sw:(q + 1) * sw] for q in range(4))
    for s in range(n_steps):
        d = 1 << s
        ar, ai = pw_ref[0, s, 0:1, :], pw_ref[0, s, 1:2, :]
        sr, si = _shift_rows(f_re, d, True), _shift_rows(f_im, d, True)
        f_re, f_im = f_re + ar * sr - ai * si, f_im + ar * si + ai * sr
        ar, ai = pw_ref[0, s, 2:3, :], pw_ref[0, s, 3:4, :]
        sr, si = _shift_rows(b_re, d, False), _shift_rows(b_im, d, False)
        b_re, b_im = b_re + ar * sr - ai * si, b_im + ar * si + ai * sr
    hprev = jnp.concatenate([_shift_rows(f_re, 1, True), _shift_rows(f_im, 1, True),
                             _shift_rows(b_re, 1, False), _shift_rows(b_im, 1, False)], axis=1)
    y = _dot(lhs, mt_ref[0]) + _dot(hprev.astype(BF16), cout_ref[0])
    for i in range(SSM_L):
        y_ref[0, pl.ds(i, nc, stride=SSM_L), :] = y[:, i * LANE:(i + 1) * LANE]


def _s5(u, w):
    b, s, _ = u.shape
    nc = s // SSM_L
    n_steps = max(1, int(math.log2(nc)))
    assert (1 << n_steps) == nc
    pw = w["ssm_pw"][:, :n_steps]
    n_blk = SSM_W // LANE
    cw = SSM_L * LANE
    once = pl.Buffered(1)
    return pl.pallas_call(
        functools.partial(_s5_kernel, n_steps=n_steps),
        grid=(n_blk, b),
        in_specs=[pl.BlockSpec((1, s, LANE), lambda gi, bi: (bi, 0, gi)),
                  pl.BlockSpec((1, cw, cw), lambda gi, bi: (gi, 0, 0), pipeline_mode=once),
                  pl.BlockSpec((1, cw, 4 * SSM_LG * SSM_P), lambda gi, bi: (gi, 0, 0), pipeline_mode=once),
                  pl.BlockSpec((1, 4 * SSM_LG * SSM_P, cw), lambda gi, bi: (gi, 0, 0), pipeline_mode=once),
                  pl.BlockSpec((1, n_steps, 4, SSM_LG * SSM_P), lambda gi, bi: (gi, 0, 0, 0))],
        out_specs=pl.BlockSpec((1, s, LANE), lambda gi, bi: (bi, 0, gi)),
        out_shape=jax.ShapeDtypeStruct((b, s, SSM_W), F32),
        compiler_params=_cparams(("arbitrary", "arbitrary")),
        name="s5",
    )(u, w["ssm_mt"], w["ssm_bin"], w["ssm_cout"], pw)


def _mla_kernel(q_ref, k_ref, v_ref, o_ref, *, tk):
    s_len = k_ref.shape[1]
    tq = q_ref.shape[1]
    qs = [q_ref[0, :, hh * HEAD_PAD:(hh + 1) * HEAD_PAD] for hh in range(2)]

    def body(j, carry):
        off = pl.multiple_of(j * tk, tk)
        new = []
        for hh in range(2):
            m, acc = carry[hh]
            sl = slice(hh * HEAD_PAD, (hh + 1) * HEAD_PAD)
            s = _dot_t(qs[hh], k_ref[0, pl.ds(off, tk), sl])
            m_new = jnp.maximum(m, jnp.max(s, axis=-1, keepdims=True))
            p = jnp.exp2(s - m_new)
            acc = jnp.exp2(m - m_new) * acc + _dot(p.astype(BF16), v_ref[0, pl.ds(off, tk), sl])
            new.append((m_new, acc))
        return tuple(new)

    init = tuple((jnp.full((tq, 1), -jnp.inf, F32), jnp.zeros((tq, HEAD_PAD), F32)) for _ in range(2))
    res = lax.fori_loop(0, s_len // tk, body, init, unroll=True)
    outs = [acc / acc[:, V_DIM:V_DIM + 1] for _, acc in res]
    lane = lax.broadcasted_iota(jnp.int32, (tq, HEAD_PAD), 1)
    o = jnp.where(lane < V_DIM, outs[0], pltpu.roll(outs[1], V_DIM, axis=1))
    o_ref[0] = o.astype(o_ref.dtype)


def _mla(q, k, v, tq, tk):
    b, s, hw = q.shape
    return pl.pallas_call(
        functools.partial(_mla_kernel, tk=tk),
        grid=(b, MLA_HEADS // 2, s // tq),
        in_specs=[pl.BlockSpec((1, tq, 2 * HEAD_PAD), lambda bi, hi, qi: (bi, qi, hi)),
                  pl.BlockSpec((1, s, 2 * HEAD_PAD), lambda bi, hi, qi: (bi, 0, hi)),
                  pl.BlockSpec((1, s, 2 * HEAD_PAD), lambda bi, hi, qi: (bi, 0, hi))],
        out_specs=pl.BlockSpec((1, tq, 2 * V_DIM), lambda bi, hi, qi: (bi, qi, hi)),
        out_shape=jax.ShapeDtypeStruct((b, s, ATT_W), BF16),
        compiler_params=_cparams(("parallel", "parallel", "arbitrary")),
        name="mla",
    )(q, k, v)


def _split_bf16(x):
    hi = x.astype(BF16)
    lo = (x - hi.astype(F32)).astype(BF16)
    return hi, lo


def _mix_kernel(x_ref, att_ref, y_ref, u_ref, d_ref, w_glu_ref, ga_ref, gs_ref, wo_a_ref, wo_s_ref,
                ln1g_ref, ln1b_ref, w_cq_ref, kc_ref, vc_ref, w_co_ref, ln2g_ref, ln2b_ref,
                wr_hi_ref, wr_lo_ref, br_ref, x2_ref, lg_ref):
    y = y_ref[...] + d_ref[...] * u_ref[...]
    gy = jax.nn.gelu(y)
    ssm = gy * jax.nn.sigmoid(_dot(gy.astype(BF16), w_glu_ref[...]))
    att = att_ref[...].astype(F32)
    mixed = (_dot(_rms(att, ga_ref[...]).astype(BF16), wo_a_ref[...])
             + _dot(_rms(ssm, gs_ref[...]).astype(BF16), wo_s_ref[...]))
    x1 = _ln(ALPHA * x_ref[...] + mixed, ln1g_ref[...], ln1b_ref[...])
    qc = (_dot(x1.astype(BF16), w_cq_ref[...]) * (XA_DIM ** -0.5)).astype(BF16)
    heads = []
    for h in range(XA_HEADS):
        sl = slice(h * XA_DIM, (h + 1) * XA_DIM)
        s = _dot_t(qc[:, sl], kc_ref[0, :, sl])
        p = jnp.exp(s - jnp.max(s, axis=-1, keepdims=True))
        o = _dot(p.astype(BF16), vc_ref[0, :, sl])
        heads.append(o / jnp.sum(p, axis=-1, keepdims=True))
    ca = _dot(jnp.concatenate(heads, axis=1).astype(BF16), w_co_ref[...])
    x2 = _ln(ALPHA * x1 + ca, ln2g_ref[...], ln2b_ref[...])
    _store_rows(x2_ref, x2)
    hi, lo = _split_bf16(x2)
    logits = (_dot(hi, wr_hi_ref[...]) + _dot(lo, wr_hi_ref[...]) + _dot(hi, wr_lo_ref[...])
              + br_ref[...])
    lg_ref[...] = _gates(logits)


def _first_max(x, lane):
    v = jnp.max(x, axis=-1, keepdims=True)
    i = jnp.min(jnp.where(x == v, lane, ROUTE_W), axis=-1, keepdims=True)
    return v, i


def _gates(logits):
    lane = lax.broadcasted_iota(jnp.int32, logits.shape, 1)
    is_grp = lane < N_GROUPS
    lg = jnp.where(is_grp, logits, -jnp.inf)
    g_max, grp = _first_max(lg, lane)
    p_grp = 1.0 / jnp.sum(jnp.where(is_grp, jnp.exp(lg - g_max), 0.0), axis=-1, keepdims=True)
    e_lane = lane - N_GROUPS
    in_grp = (e_lane >= 0) & (e_lane < N_EXPERTS) & ((e_lane // EXP_PER_GROUP) == grp)
    le = jnp.where(in_grp, logits, -jnp.inf)
    v1, i1 = _first_max(le, lane)
    v2, i2 = _first_max(jnp.where(lane == i1, -jnp.inf, le), lane)
    e21 = jnp.exp(v2 - v1)
    w1 = p_grp / (1.0 + e21)
    w2 = w1 * e21
    out = jnp.where(lane == 0, (i1 - N_GROUPS).astype(F32),
                    jnp.where(lane == 1, (i2 - N_GROUPS).astype(F32),
                              jnp.where(lane == 2, w1, jnp.where(lane == 3, w2, 0.0))))
    return out


def _mix(x2d, att, y, u, kvc, w, seq, tm):
    t = x2d.shape[0]
    nseq = seq // tm
    n_mem = kvc.shape[1]
    const = lambda i: (0, 0)
    row = lambda i: (i, 0)
    mem_k = lambda i: (i // nseq, 0, 0)
    mem_v = lambda i: (i // nseq, 0, 1)
    vec = lambda n: pl.BlockSpec((1, n), const)
    return pl.pallas_call(
        _mix_kernel,
        grid=(t // tm,),
        in_specs=[pl.BlockSpec((tm, D_MODEL), row),
                  pl.BlockSpec((tm, ATT_W), row),
                  pl.BlockSpec((tm, SSM_W), row),
                  pl.BlockSpec((tm, SSM_W), row),
                  vec(SSM_W),
                  pl.BlockSpec((SSM_W, SSM_W), const),
                  vec(ATT_W), vec(SSM_W),
                  pl.BlockSpec((ATT_W, D_MODEL), const),
                  pl.BlockSpec((SSM_W, D_MODEL), const),
                  vec(D_MODEL), vec(D_MODEL),
                  pl.BlockSpec((D_MODEL, D_MODEL), const),
                  pl.BlockSpec((1, n_mem, D_MODEL), mem_k),
                  pl.BlockSpec((1, n_mem, D_MODEL), mem_v),
                  pl.BlockSpec((D_MODEL, D_MODEL), const),
                  vec(D_MODEL), vec(D_MODEL),
                  pl.BlockSpec((D_MODEL, ROUTE_W), const),
                  pl.BlockSpec((D_MODEL, ROUTE_W), const),
                  vec(ROUTE_W)],
        out_specs=[pl.BlockSpec((tm * ROW_SUB, LANE), row),
                   pl.BlockSpec((tm, ROUTE_W), row)],
        out_shape=[jax.ShapeDtypeStruct((t * ROW_SUB, LANE), F32),
                   jax.ShapeDtypeStruct((t, ROUTE_W), F32)],
        compiler_params=_cparams(("parallel",)),
        name="mix",
    )(x2d, att, y, u, w["ssm_d"], w["w_glu"], w["g_attn"], w["g_ssm"], w["w_out_a"], w["w_out_s"],
      w["ln1_g"], w["ln1_b"], w["w_cq"], kvc, kvc, w["w_co"], w["ln2_g"], w["ln2_b"],
      w["wr_hi"], w["wr_lo"], w["b_r"])


def _row_copy(src_hbm, dst_vmem, sem, src_row, dst_row):
    return pltpu.make_async_copy(src_hbm.at[pl.ds(pl.multiple_of(src_row * ROW_SUB, ROW_SUB), ROW_SUB)],
                                 dst_vmem.at[pl.ds(pl.multiple_of(dst_row * ROW_SUB, ROW_SUB), ROW_SUB)], sem)


def _load_rows(ref, n_rows):
    return jnp.concatenate([ref[pl.ds(j, n_rows, stride=ROW_SUB), :] for j in range(ROW_SUB)], axis=1)


def _store_rows(ref, val):
    n_rows = val.shape[0]
    for j in range(ROW_SUB):
        ref[pl.ds(j, n_rows, stride=ROW_SUB), :] = val[:, j * LANE:(j + 1) * LANE]


def _gather_rows(src_hbm, idx_smem, dst_vmem, sem, n_rows):
    def issue(r, c):
        _row_copy(src_hbm, dst_vmem, sem, idx_smem[r], r).start()
        return c

    lax.fori_loop(0, n_rows, issue, 0, unroll=GATHER_UNROLL)


def _wait_rows(src_hbm, dst_vmem, sem, n_rows):
    def drain(r, c):
        _row_copy(src_hbm, dst_vmem, sem, 0, r).wait()
        return c

    lax.fori_loop(0, n_rows, drain, 0, unroll=GATHER_UNROLL)


def _fetch_tile(i, idx_hbm, idx_smem, isem, src_hbm, buf, gsem, n_rows):
    cp = pltpu.make_async_copy(idx_hbm.at[i], idx_smem, isem)
    cp.start()
    cp.wait()
    _gather_rows(src_hbm, idx_smem, buf, gsem, n_rows)
    _wait_rows(src_hbm, buf, gsem, n_rows)


def _experts_kernel(te_ref, nv_ref, src_hbm, x_hbm, gate_ref, w1_ref, w3_ref, w2_ref, y_ref,
                    idx_smem, xbuf, isem, gsem):
    i = pl.program_id(0)
    n = nv_ref[0]
    tm = xbuf.shape[0] // ROW_SUB

    @pl.when(i < n)
    def _():
        _fetch_tile(i, src_hbm, idx_smem, isem, x_hbm, xbuf, gsem, tm)
        xb = _load_rows(xbuf, tm).astype(BF16)
        hdn = jax.nn.silu(_dot(xb, w1_ref[0])) * _dot(xb, w3_ref[0])
        _store_rows(y_ref, gate_ref[...] * _dot(hdn.astype(BF16), w2_ref[0]))

    @pl.when(i >= n)
    def _():
        y_ref[...] = jnp.zeros(y_ref.shape, y_ref.dtype)


def _experts(x2, row_src, row_gate, tile_expert, n_valid, w, tm):
    n_tiles = row_src.shape[0]
    ew = lambda i, te, nv: (te[i], 0, 0)
    grid_spec = pltpu.PrefetchScalarGridSpec(
        num_scalar_prefetch=2,
        grid=(n_tiles,),
        in_specs=[pl.BlockSpec(memory_space=pl.ANY),
                  pl.BlockSpec(memory_space=pl.ANY),
                  pl.BlockSpec((tm, 1), lambda i, te, nv: (i, 0)),
                  pl.BlockSpec((1, D_MODEL, D_FF_EXPERT), ew),
                  pl.BlockSpec((1, D_MODEL, D_FF_EXPERT), ew),
                  pl.BlockSpec((1, D_FF_EXPERT, D_MODEL), ew)],
        out_specs=pl.BlockSpec((tm * ROW_SUB, LANE), lambda i, te, nv: (i, 0)),
        scratch_shapes=[pltpu.SMEM((tm,), jnp.int32),
                        pltpu.VMEM((tm * ROW_SUB, LANE), F32),
                        pltpu.SemaphoreType.DMA,
                        pltpu.SemaphoreType.DMA],
    )
    return pl.pallas_call(
        _experts_kernel,
        grid_spec=grid_spec,
        out_shape=jax.ShapeDtypeStruct((n_tiles * tm * ROW_SUB, LANE), F32),
        compiler_params=_cparams(("arbitrary",)),
        name="experts",
    )(tile_expert, n_valid, row_src, x2, row_gate, w["w_e1"], w["w_e3"], w["w_e2"])


def _combine_kernel(pos_hbm, x2_ref, y_hbm, g_ref, b_ref, o_ref, idx_smem, ybuf, isem, gsem):
    i = pl.program_id(0)
    tm = o_ref.shape[0]
    _fetch_tile(i, pos_hbm, idx_smem, isem, y_hbm, ybuf, gsem, TOP_K * tm)
    rows = _load_rows(ybuf, TOP_K * tm)
    moe = rows[0:tm] + rows[tm:2 * tm]
    o_ref[...] = _ln(ALPHA * _load_rows(x2_ref, tm) + moe, g_ref[...], b_ref[...])


def _combine(x2, y_sorted, pos_tiles, w, tm):
    t = x2.shape[0] // ROW_SUB
    const = lambda i: (0, 0)
    row = lambda i: (i, 0)
    return pl.pallas_call(
        _combine_kernel,
        grid=(t // tm,),
        in_specs=[pl.BlockSpec(memory_space=pl.ANY),
                  pl.BlockSpec((tm * ROW_SUB, LANE), row),
                  pl.BlockSpec(memory_space=pl.ANY),
                  pl.BlockSpec((1, D_MODEL), const),
                  pl.BlockSpec((1, D_MODEL), const)],
        out_specs=pl.BlockSpec((tm, D_MODEL), row),
        out_shape=jax.ShapeDtypeStruct((t, D_MODEL), F32),
        scratch_shapes=[pltpu.SMEM((TOP_K * tm,), jnp.int32),
                        pltpu.VMEM((TOP_K * tm * ROW_SUB, LANE), F32),
                        pltpu.SemaphoreType.DMA,
                        pltpu.SemaphoreType.DMA],
        compiler_params=_cparams(("arbitrary",)),
        name="combine",
    )(pos_tiles, x2, y_sorted, w["ln3_g"], w["ln3_b"])


def _route(gates, tm_e, tm_c):
    t = gates.shape[0]
    flat_e = gates[:, :TOP_K].astype(jnp.int32).reshape(-1)
    flat_w = gates[:, TOP_K:2 * TOP_K].reshape(-1)
    n = flat_e.shape[0]
    ids = jnp.arange(n, dtype=jnp.int32)
    sorted_e, order = lax.sort((flat_e, ids), num_keys=1, is_stable=True)
    _, inv = lax.sort((order, ids), num_keys=1)
    start_unp = jnp.searchsorted(sorted_e, jnp.arange(N_EXPERTS + 1, dtype=jnp.int32), side="left").astype(jnp.int32)
    counts = start_unp[1:] - start_unp[:-1]
    start_unp = start_unp[:-1]
    padded = ((counts + tm_e - 1) // tm_e) * tm_e
    start_pad = jnp.cumsum(padded) - padded
    pos = (start_pad[flat_e] + inv - start_unp[flat_e]).astype(jnp.int32)

    n_tiles = n // tm_e + N_EXPERTS
    rows = jnp.arange(n_tiles * tm_e, dtype=jnp.int32)
    end_pad = start_pad + padded
    tile_expert = jnp.minimum(jnp.sum(end_pad[:, None] <= rows[None, ::tm_e], axis=0), N_EXPERTS - 1).astype(jnp.int32)
    row_e = jnp.broadcast_to(tile_expert[:, None], (n_tiles, tm_e)).reshape(-1)
    within = rows - start_pad[row_e]
    valid = within < counts[row_e]
    src = order[jnp.clip(start_unp[row_e] + within, 0, n - 1)]
    row_src = jnp.where(valid, src // TOP_K, 0).astype(jnp.int32)
    row_gate = jnp.where(valid, flat_w[src], 0.0).astype(F32)
    n_valid = (jnp.sum(padded) // tm_e).astype(jnp.int32).reshape(1)
    pos_tiles = pos.reshape(t // tm_c, tm_c, TOP_K).transpose(0, 2, 1).reshape(t // tm_c, TOP_K * tm_c)
    return (row_src.reshape(n_tiles, tm_e), row_gate.reshape(n_tiles * tm_e, 1), tile_expert, n_valid,
            pos_tiles)


def _rope_tables(seq):
    inv = ROPE_THETA ** (-jnp.arange(0, ROPE_DIM, 2, dtype=F32) / ROPE_DIM)
    ang = jnp.arange(seq, dtype=F32)[:, None] * inv[None, :]
    c, s = jnp.cos(ang), jnp.sin(ang)
    z = jnp.zeros((seq, HEAD_PAD - NOPE_DIM - ROPE_DIM), F32)
    cos_t = jnp.concatenate([jnp.ones((seq, NOPE_DIM), F32), c, c, z], axis=1)
    sin_t = jnp.concatenate([jnp.zeros((seq, NOPE_DIM), F32), s, s, z], axis=1)
    return cos_t, sin_t


def _pair_rot(w_rope):
    half = ROPE_DIM // 2
    return jnp.concatenate([-w_rope[..., half:], w_rope[..., :half]], axis=-1)


def _ssm_operators(a_re, a_im, log_dt, b_re, b_im, c_re, c_im, max_steps):
    a = lax.complex(a_re, a_im)
    dt = jnp.exp(log_dt)[..., None]
    adt = a * dt
    a_bar = jnp.exp(adt)
    b_bar = ((a_bar - 1.0) / a)[..., None] * lax.complex(b_re, b_im)
    cm = lax.complex(c_re, c_im)
    taus = jnp.arange(SSM_L + 1, dtype=F32)
    pw = jnp.exp(adt[None] * taus[:, None, None, None])
    kf = jnp.real(jnp.einsum("gcp,tgp,gpd->tgcd", cm[0], pw[:SSM_L, 0], b_bar[0]))
    kb = jnp.real(jnp.einsum("gcp,tgp,gpd->tgcd", cm[1], pw[:SSM_L, 1], b_bar[1]))
    kfull = jnp.concatenate([kb[:0:-1], kf[:1] + kb[:1], kf[1:]], axis=0)
    ii = jnp.arange(SSM_L)
    lag = ii[:, None] - ii[None, :] + SSM_L - 1
    mt = kfull[lag]
    mt = mt.transpose(2, 1, 4, 0, 3)
    pf = pw[SSM_L - 1 - ii, 0]
    pb = pw[ii, 1]
    sf = pf[..., None] * b_bar[0][None]
    sb = pb[..., None] * b_bar[1][None]
    bin_ = jnp.stack([jnp.real(sf), jnp.imag(sf), jnp.real(sb), jnp.imag(sb)], axis=2)
    bin_ = bin_.transpose(1, 0, 4, 2, 3)
    of = cm[0][None] * pw[1 + ii, 0][:, :, None, :]
    ob = cm[1][None] * pw[SSM_L - ii, 1][:, :, None, :]
    cout = jnp.stack([jnp.real(of), -jnp.imag(of), jnp.real(ob), -jnp.imag(ob)], axis=3)
    cout = cout.transpose(1, 3, 4, 0, 2)
    n_blk = SSM_G // SSM_LG
    eye = jnp.eye(SSM_LG, dtype=BF16)
    blk = lambda t: t.astype(BF16).reshape((n_blk, SSM_LG) + t.shape[1:])
    cw = SSM_L * LANE
    sw4 = 4 * SSM_LG * SSM_P
    mt_b = jnp.einsum("gh,ngjdic->njgdihc", eye, blk(mt)).reshape(n_blk, cw, cw)
    bin_b = jnp.einsum("gh,ngjdqp->njgdqhp", eye, blk(bin_)).reshape(n_blk, cw, sw4)
    cout_b = jnp.einsum("gh,ngqpic->nqgpihc", eye, blk(cout)).reshape(n_blk, sw4, cw)
    steps = (SSM_L * (2.0 ** jnp.arange(max_steps, dtype=F32)))
    sp = jnp.exp(adt[None] * steps[:, None, None, None])
    tab = jnp.stack([jnp.real(sp[:, 0]), jnp.imag(sp[:, 0]), jnp.real(sp[:, 1]), jnp.imag(sp[:, 1])],
                    axis=1)
    tab = tab.reshape(max_steps, 4, n_blk, SSM_LG * SSM_P).transpose(2, 0, 1, 3)
    return mt_b, bin_b, cout_b, tab.astype(F32)


def _prepare(p, max_chunks):
    w = {}
    w_in = p["w_in"]
    o_kpe = Q_LORA + KV_LORA
    kpe = w_in[:, o_kpe:o_kpe + ROPE_DIM]
    kpe_blk = jnp.concatenate([jnp.zeros((D_MODEL, NOPE_DIM), F32), kpe, _pair_rot(kpe)], axis=1)
    w["w_in"] = jnp.concatenate([w_in[:, :o_kpe], w_in[:, o_kpe + ROPE_DIM:], kpe_blk], axis=1).astype(BF16)
    w["g_q"] = p["g_q_lat"].reshape(1, -1)
    w["g_kv"] = p["g_kv_lat"].reshape(1, -1)
    uq = p["w_uq"].reshape(Q_LORA, MLA_HEADS, QK_DIM)
    rope = uq[..., NOPE_DIM:]
    w["w_uq"] = jnp.concatenate([uq, _pair_rot(rope)], axis=-1).reshape(Q_LORA, -1).astype(BF16)
    ukv = p["w_ukv"].reshape(KV_LORA, MLA_HEADS, NOPE_DIM + V_DIM)
    zpad = jnp.zeros((KV_LORA, MLA_HEADS, HEAD_PAD - NOPE_DIM), F32)
    w["w_kn"] = jnp.concatenate([ukv[..., :NOPE_DIM], zpad], axis=-1).reshape(KV_LORA, -1).astype(BF16)
    w["w_v"] = jnp.concatenate([ukv[..., NOPE_DIM:], zpad], axis=-1).reshape(KV_LORA, -1).astype(BF16)
    one = jnp.zeros((MLA_HEADS, HEAD_PAD), F32).at[:, V_DIM].set(1.0)
    w["v_one"] = one.reshape(1, -1)
    max_steps = max(1, int(math.log2(max_chunks)))
    w["ssm_mt"], w["ssm_bin"], w["ssm_cout"], w["ssm_pw"] = _ssm_operators(
        p["ssm_a_re"], p["ssm_a_im"], p["ssm_log_dt"], p["ssm_b_re"], p["ssm_b_im"],
        p["ssm_c_re"], p["ssm_c_im"], max_steps)
    w["ssm_d"] = p["ssm_d"].reshape(1, -1)
    w["w_glu"] = p["w_glu"].astype(BF16)
    w["g_attn"] = p["g_attn_grp"].reshape(1, -1)
    w["g_ssm"] = p["g_ssm_grp"].reshape(1, -1)
    w["w_out_a"] = p["w_out"][:ATT_W].astype(BF16)
    w["w_out_s"] = p["w_out"][ATT_W:].astype(BF16)
    for n in ("ln1_g", "ln1_b", "ln2_g", "ln2_b", "ln3_g", "ln3_b"):
        w[n] = p[n].reshape(1, -1)
    w["w_cq"] = p["w_cq"].astype(BF16)
    w["w_ckv"] = p["w_ckv"].astype(BF16)
    w["w_co"] = p["w_co"].astype(BF16)
    wr = jnp.concatenate([p["w_rg"], p["w_re"],
                          jnp.zeros((D_MODEL, ROUTE_W - N_GROUPS - N_EXPERTS), F32)], axis=1)
    w["wr_hi"], w["wr_lo"] = _split_bf16(wr)
    w["b_r"] = jnp.concatenate([p["b_rg"], p["b_re"],
                                jnp.zeros((ROUTE_W - N_GROUPS - N_EXPERTS,), F32)]).reshape(1, -1)
    w["w_e1"] = p["w_e1"].astype(BF16)
    w["w_e3"] = p["w_e3"].astype(BF16)
    w["w_e2"] = p["w_e2"].astype(BF16)
    return w


def _tile(n, pref):
    return pref if n % pref == 0 else n


def _run(x, mem, w):
    b, s, _ = x.shape
    t = b * s
    n_mem = mem.shape[1]
    x2d = x.reshape(t, D_MODEL)
    tm = _tile(s, TOKEN_TILE)
    scale = QK_DIM ** -0.5 * math.log2(math.e)
    cos_t, sin_t = _rope_tables(s)
    wl = dict(w, cos_q=cos_t * scale, sin_q=sin_t * scale, cos_k=cos_t, sin_k=sin_t)

    kvc = _matmul(mem.reshape(b * n_mem, D_MODEL), w["w_ckv"], BF16, n_mem, D_MODEL)
    kvc = kvc.reshape(b, n_mem, 2 * D_MODEL)

    q, k, v, u = _in_proj(x2d, wl, s, tm)
    hw = MLA_HEADS * HEAD_PAD
    att = _mla(q.reshape(b, s, hw), k.reshape(b, s, hw), v.reshape(b, s, hw), tm, s)

    y = _s5(u.reshape(b, s, SSM_W), w).reshape(t, SSM_W)

    x2, gates = _mix(x2d, att.reshape(t, ATT_W), y, u, kvc, w, s, tm)

    tm_e = EXPERT_TILE
    tm_c = _tile(t, TOKEN_TILE)
    row_src, row_gate, tile_expert, n_valid, pos_tiles = _route(gates, tm_e, tm_c)
    y_sorted = _experts(x2, row_src, row_gate, tile_expert, n_valid, w, tm_e)
    out = _combine(x2, y_sorted, pos_tiles, w, tm_c)
    return out.reshape(b, s, D_MODEL)


def kernel(x_prompt, x_sample, mem_prompt, mem_sample, w_in, g_q_lat, w_uq, g_kv_lat, w_ukv, ssm_a_re, ssm_a_im, ssm_log_dt, ssm_b_re, ssm_b_im, ssm_c_re, ssm_c_im, ssm_d, w_glu, g_attn_grp, g_ssm_grp, w_out, ln1_g, ln1_b, w_cq, w_ckv, w_co, ln2_g, ln2_b, w_rg, b_rg, w_re, b_re, w_e1, w_e3, w_e2, ln3_g, ln3_b):
    params = dict(w_in=w_in, g_q_lat=g_q_lat, w_uq=w_uq, g_kv_lat=g_kv_lat, w_ukv=w_ukv,
                  ssm_a_re=ssm_a_re, ssm_a_im=ssm_a_im, ssm_log_dt=ssm_log_dt, ssm_b_re=ssm_b_re,
                  ssm_b_im=ssm_b_im, ssm_c_re=ssm_c_re, ssm_c_im=ssm_c_im, ssm_d=ssm_d, w_glu=w_glu,
                  g_attn_grp=g_attn_grp, g_ssm_grp=g_ssm_grp, w_out=w_out, ln1_g=ln1_g, ln1_b=ln1_b,
                  w_cq=w_cq, w_ckv=w_ckv, w_co=w_co, ln2_g=ln2_g, ln2_b=ln2_b,
                  w_rg=w_rg, b_rg=b_rg, w_re=w_re, b_re=b_re, w_e1=w_e1, w_e3=w_e3, w_e2=w_e2,
                  ln3_g=ln3_g, ln3_b=ln3_b)
    params = {name: val[0] for name, val in params.items()}
    max_chunks = max(x_prompt.shape[1], x_sample.shape[1]) // SSM_L
    w = _prepare(params, max_chunks)
    return (_run(x_prompt, mem_prompt, w), _run(x_sample, mem_sample, w))
```

```python
import functools
import math

import jax
import jax.numpy as jnp
from jax import lax
from jax.experimental import pallas as pl
from jax.experimental.pallas import tpu as pltpu

F32 = jnp.float32
BF16 = jnp.bfloat16

D_MODEL = 1024
MLA_HEADS = 8
NOPE_DIM = 64
ROPE_DIM = 32
V_DIM = 64
QK_DIM = NOPE_DIM + ROPE_DIM
Q_LORA = 384
KV_LORA = 256
ATT_W = MLA_HEADS * V_DIM
ROPE_THETA = 10000.0
SSM_W = D_MODEL - ATT_W
SSM_GC = 16
SSM_G = SSM_W // SSM_GC
SSM_P = 64
XA_HEADS = 4
XA_DIM = D_MODEL // XA_HEADS
N_GROUPS = 4
EXP_PER_GROUP = 8
N_EXPERTS = N_GROUPS * EXP_PER_GROUP
TOP_K = 2
D_FF_EXPERT = 256
DEPTH = 1
ALPHA = (2 * DEPTH) ** 0.25
LN_EPS = 1e-5
RMS_EPS = 1e-6

LANE = 128
HEAD_PAD = 128
SSM_L = 16
SSM_LG = LANE // SSM_GC
ROUTE_W = 128
ROW_SUB = D_MODEL // LANE
GATHER_UNROLL = 8
TOKEN_TILE = 512
EXPERT_TILE = 512
VMEM_LIMIT = 56 * 1024 * 1024


def _cparams(sem):
    return pltpu.CompilerParams(dimension_semantics=sem, vmem_limit_bytes=VMEM_LIMIT)


def _rms(x, g):
    return x * lax.rsqrt(jnp.mean(x * x, axis=-1, keepdims=True) + RMS_EPS) * g


def _ln(x, g, b):
    mu = jnp.mean(x, axis=-1, keepdims=True)
    xc = x - mu
    var = jnp.mean(xc * xc, axis=-1, keepdims=True)
    return xc * lax.rsqrt(var + LN_EPS) * g + b


def _dot(a, b):
    return jnp.dot(a, b, preferred_element_type=F32)


def _dot_t(a, b):
    return lax.dot_general(a, b, (((1,), (1,)), ((), ())), preferred_element_type=F32)


def _matmul_kernel(x_ref, w_ref, o_ref):
    o_ref[...] = _dot(x_ref[...].astype(BF16), w_ref[...]).astype(o_ref.dtype)


def _matmul(x, w, out_dtype, tm, tn):
    m, k = x.shape
    n = w.shape[1]
    return pl.pallas_call(
        _matmul_kernel,
        grid=(m // tm, n // tn),
        in_specs=[pl.BlockSpec((tm, k), lambda i, j: (i, 0)),
                  pl.BlockSpec((k, tn), lambda i, j: (0, j))],
        out_specs=pl.BlockSpec((tm, tn), lambda i, j: (i, j)),
        out_shape=jax.ShapeDtypeStruct((m, n), out_dtype),
        compiler_params=_cparams(("parallel", "parallel")),
        name="mem_kv",
    )(x, w)


def _rot(blk, cos, sin):
    return blk * cos + pltpu.roll(blk, HEAD_PAD - ROPE_DIM, axis=1) * sin


def _in_proj_kernel(x_ref, w_in_ref, gq_ref, gkv_ref, w_uq_ref, w_kn_ref, w_v_ref, vone_ref,
                    cq_ref, sq_ref, ck_ref, sk_ref,
                    q_ref, k_ref, v_ref, u_ref):
    xb = x_ref[...].astype(BF16)
    proj = _dot(xb, w_in_ref[...])
    o_kv = Q_LORA
    o_u = Q_LORA + KV_LORA
    o_kr = o_u + SSM_W
    u_ref[...] = proj[:, o_u:o_kr]
    qn = _rms(proj[:, :Q_LORA], gq_ref[...]).astype(BF16)
    kvn = _rms(proj[:, o_kv:o_u], gkv_ref[...]).astype(BF16)
    qa = _dot(qn, w_uq_ref[...])
    kn = _dot(kvn, w_kn_ref[...])
    v = _dot(kvn, w_v_ref[...]) + vone_ref[...]
    v_ref[...] = v.astype(BF16)
    cq, sq, ck, sk = cq_ref[...], sq_ref[...], ck_ref[...], sk_ref[...]
    kr = _rot(proj[:, o_kr:o_kr + HEAD_PAD], ck, sk)
    for h in range(MLA_HEADS):
        sl = slice(h * HEAD_PAD, (h + 1) * HEAD_PAD)
        q_ref[:, sl] = _rot(qa[:, sl], cq, sq).astype(BF16)
        k_ref[:, sl] = (kn[:, sl] + kr).astype(BF16)


def _in_proj(x2d, w, seq, tm):
    t = x2d.shape[0]
    nseq = seq // tm
    hw = MLA_HEADS * HEAD_PAD
    const = lambda i: (0, 0)
    row = lambda i: (i, 0)
    pos = lambda i: (i % nseq, 0)
    n_in = w["w_in"].shape[1]
    return pl.pallas_call(
        _in_proj_kernel,
        grid=(t // tm,),
        in_specs=[pl.BlockSpec((tm, D_MODEL), row),
                  pl.BlockSpec((D_MODEL, n_in), const),
                  pl.BlockSpec((1, Q_LORA), const),
                  pl.BlockSpec((1, KV_LORA), const),
                  pl.BlockSpec((Q_LORA, hw), const),
                  pl.BlockSpec((KV_LORA, hw), const),
                  pl.BlockSpec((KV_LORA, hw), const),
                  pl.BlockSpec((1, hw), const),
                  pl.BlockSpec((tm, HEAD_PAD), pos),
                  pl.BlockSpec((tm, HEAD_PAD), pos),
                  pl.BlockSpec((tm, HEAD_PAD), pos),
                  pl.BlockSpec((tm, HEAD_PAD), pos)],
        out_specs=[pl.BlockSpec((tm, hw), row),
                   pl.BlockSpec((tm, hw), row),
                   pl.BlockSpec((tm, hw), row),
                   pl.BlockSpec((tm, SSM_W), row)],
        out_shape=[jax.ShapeDtypeStruct((t, hw), BF16),
                   jax.ShapeDtypeStruct((t, hw), BF16),
                   jax.ShapeDtypeStruct((t, hw), BF16),
                   jax.ShapeDtypeStruct((t, SSM_W), F32)],
        compiler_params=_cparams(("parallel",)),
        name="in_proj",
    )(x2d, w["w_in"], w["g_q"], w["g_kv"], w["w_uq"], w["w_kn"], w["w_v"], w["v_one"],
      w["cos_q"], w["sin_q"], w["cos_k"], w["sin_k"])


def _shift_rows(x, d, down):
    n = x.shape[0]
    r = lax.broadcasted_iota(jnp.int32, x.shape, 0)
    if down:
        return jnp.where(r >= d, pltpu.roll(x, d, axis=0), 0.0)
    return jnp.where(r < n - d, pltpu.roll(x, n - d, axis=0), 0.0)


def _s5_kernel(u_ref, mt_ref, bin_ref, cout_ref, pw_ref, y_ref, *, n_steps):
    nc = u_ref.shape[1] // SSM_L
    sw = SSM_LG * SSM_P
    lhs = jnp.concatenate([u_ref[0, pl.ds(s, nc, stride=SSM_L), :].astype(BF16) for s in range(SSM_L)],
                          axis=1)
    hin = _dot(lhs, bin_ref[0])
    f_re, f_im, b_re, b_im = (hin[:, q * sw:(q + 1) * sw] for q in range(4))
    for s in range(n_steps):
        d = 1 << s
        ar, ai = pw_ref[0, s, 0:1, :], pw_ref[0, s, 1:2, :]
        sr, si = _shift_rows(f_re, d, True), _shift_rows(f_im, d, True)
        f_re, f_im = f_re + ar * sr - ai * si, f_im + ar * si + ai * sr
        ar, ai = pw_ref[0, s, 2:3, :], pw_ref[0, s, 3:4, :]
        sr, si = _shift_rows(b_re, d, False), _shift_rows(b_im, d, False)
        b_re, b_im = b_re + ar * sr - ai * si, b_im + ar * si + ai * sr
    hprev = jnp.concatenate([_shift_rows(f_re, 1, True), _shift_rows(f_im, 1, True),
                             _shift_rows(b_re, 1, False), _shift_rows(b_im, 1, False)], axis=1)
    y = _dot(lhs, mt_ref[0]) + _dot(hprev.astype(BF16), cout_ref[0])
    for i in range(SSM_L):
        y_ref[0, pl.ds(i, nc, stride=SSM_L), :] = y[:, i * LANE:(i + 1) * LANE]


def _s5(u, w):
    b, s, _ = u.shape
    nc = s // SSM_L
    n_steps = max(1, int(math.log2(nc)))
    assert (1 << n_steps) == nc
    pw = w["ssm_pw"][:, :n_steps]
    n_blk = SSM_W // LANE
    cw = SSM_L * LANE
    once = pl.Buffered(1)
    return pl.pallas_call(
        functools.partial(_s5_kernel, n_steps=n_steps),
        grid=(n_blk, b),
        in_specs=[pl.BlockSpec((1, s, LANE), lambda gi, bi: (bi, 0, gi)),
                  pl.BlockSpec((1, cw, cw), lambda gi, bi: (gi, 0, 0), pipeline_mode=once),
                  pl.BlockSpec((1, cw, 4 * SSM_LG * SSM_P), lambda gi, bi: (gi, 0, 0), pipeline_mode=once),
                  pl.BlockSpec((1, 4 * SSM_LG * SSM_P, cw), lambda gi, bi: (gi, 0, 0), pipeline_mode=once),
                  pl.BlockSpec((1, n_steps, 4, SSM_LG * SSM_P), lambda gi, bi: (gi, 0, 0, 0))],
        out_specs=pl.BlockSpec((1, s, LANE), lambda gi, bi: (bi, 0, gi)),
        out_shape=jax.ShapeDtypeStruct((b, s, SSM_W), F32),
        compiler_params=_cparams(("arbitrary", "arbitrary")),
        name="s5",
    )(u, w["ssm_mt"], w["ssm_bin"], w["ssm_cout"], pw)


def _mla_kernel(q_ref, k_ref, v_ref, o_ref, *, tk):
    s_len = k_ref.shape[1]
    tq = q_ref.shape[1]
    qs = [q_ref[0, :, hh * HEAD_PAD:(hh + 1) * HEAD_PAD] for hh in range(2)]

    def body(j, carry):
        off = pl.multiple_of(j * tk, tk)
        new = []
        for hh in range(2):
            m, acc = carry[hh]
            sl = slice(hh * HEAD_PAD, (hh + 1) * HEAD_PAD)
            s = _dot_t(qs[hh], k_ref[0, pl.ds(off, tk), sl])
            m_new = jnp.maximum(m, jnp.max(s, axis=-1, keepdims=True))
            p = jnp.exp2(s - m_new)
            acc = jnp.exp2(m - m_new) * acc + _dot(p.astype(BF16), v_ref[0, pl.ds(off, tk), sl])
            new.append((m_new, acc))
        return tuple(new)

    init = tuple((jnp.full((tq, 1), -jnp.inf, F32), jnp.zeros((tq, HEAD_PAD), F32)) for _ in range(2))
    res = lax.fori_loop(0, s_len // tk, body, init, unroll=True)
    outs = [acc / acc[:, V_DIM:V_DIM + 1] for _, acc in res]
    lane = lax.broadcasted_iota(jnp.int32, (tq, HEAD_PAD), 1)
    o = jnp.where(lane < V_DIM, outs[0], pltpu.roll(outs[1], V_DIM, axis=1))
    o_ref[0] = o.astype(o_ref.dtype)


def _mla(q, k, v, tq, tk):
    b, s, hw = q.shape
    return pl.pallas_call(
        functools.partial(_mla_kernel, tk=tk),
        grid=(b, MLA_HEADS // 2, s // tq),
        in_specs=[pl.BlockSpec((1, tq, 2 * HEAD_PAD), lambda bi, hi, qi: (bi, qi, hi)),
                  pl.BlockSpec((1, s, 2 * HEAD_PAD), lambda bi, hi, qi: (bi, 0, hi)),
                  pl.BlockSpec((1, s, 2 * HEAD_PAD), lambda bi, hi, qi: (bi, 0, hi))],
        out_specs=pl.BlockSpec((1, tq, 2 * V_DIM), lambda bi, hi, qi: (bi, qi, hi)),
        out_shape=jax.ShapeDtypeStruct((b, s, ATT_W), BF16),
        compiler_params=_cparams(("parallel", "parallel", "arbitrary")),
        name="mla",
    )(q, k, v)


def _split_bf16(x):
    hi = x.astype(BF16)
    lo = (x - hi.astype(F32)).astype(BF16)
    return hi, lo


def _mix_kernel(x_ref, att_ref, y_ref, u_ref, d_ref, w_glu_ref, ga_ref, gs_ref, wo_a_ref, wo_s_ref,
                ln1g_ref, ln1b_ref, w_cq_ref, kc_ref, vc_ref, w_co_ref, ln2g_ref, ln2b_ref,
                wr_hi_ref, wr_lo_ref, br_ref, x2_ref, lg_ref):
    y = y_ref[...] + d_ref[...] * u_ref[...]
    gy = jax.nn.gelu(y)
    ssm = gy * jax.nn.sigmoid(_dot(gy.astype(BF16), w_glu_ref[...]))
    att = att_ref[...].astype(F32)
    mixed = (_dot(_rms(att, ga_ref[...]).astype(BF16), wo_a_ref[...])
             + _dot(_rms(ssm, gs_ref[...]).astype(BF16), wo_s_ref[...]))
    x1 = _ln(ALPHA * x_ref[...] + mixed, ln1g_ref[...], ln1b_ref[...])
    qc = (_dot(x1.astype(BF16), w_cq_ref[...]) * (XA_DIM ** -0.5)).astype(BF16)
    heads = []
    for h in range(XA_HEADS):
        sl = slice(h * XA_DIM, (h + 1) * XA_DIM)
        s = _dot_t(qc[:, sl], kc_ref[0, :, sl])
        p = jnp.exp(s - jnp.max(s, axis=-1, keepdims=True))
        o = _dot(p.astype(BF16), vc_ref[0, :, sl])
        heads.append(o / jnp.sum(p, axis=-1, keepdims=True))
    ca = _dot(jnp.concatenate(heads, axis=1).astype(BF16), w_co_ref[...])
    x2 = _ln(ALPHA * x1 + ca, ln2g_ref[...], ln2b_ref[...])
    _store_rows(x2_ref, x2)
    hi, lo = _split_bf16(x2)
    logits = (_dot(hi, wr_hi_ref[...]) + _dot(lo, wr_hi_ref[...]) + _dot(hi, wr_lo_ref[...])
              + br_ref[...])
    lg_ref[...] = _gates(logits)


def _first_max(x, lane):
    v = jnp.max(x, axis=-1, keepdims=True)
    i = jnp.min(jnp.where(x == v, lane, ROUTE_W), axis=-1, keepdims=True)
    return v, i


def _gates(logits):
    lane = lax.broadcasted_iota(jnp.int32, logits.shape, 1)
    is_grp = lane < N_GROUPS
    lg = jnp.where(is_grp, logits, -jnp.inf)
    g_max, grp = _first_max(lg, lane)
    p_grp = 1.0 / jnp.sum(jnp.where(is_grp, jnp.exp(lg - g_max), 0.0), axis=-1, keepdims=True)
    e_lane = lane - N_GROUPS
    in_grp = (e_lane >= 0) & (e_lane < N_EXPERTS) & ((e_lane // EXP_PER_GROUP) == grp)
    le = jnp.where(in_grp, logits, -jnp.inf)
    v1, i1 = _first_max(le, lane)
    v2, i2 = _first_max(jnp.where(lane == i1, -jnp.inf, le), lane)
    e21 = jnp.exp(v2 - v1)
    w1 = p_grp / (1.0 + e21)
    w2 = w1 * e21
    out = jnp.where(lane == 0, (i1 - N_GROUPS).astype(F32),
                    jnp.where(lane == 1, (i2 - N_GROUPS).astype(F32),
                              jnp.where(lane == 2, w1, jnp.where(lane == 3, w2, 0.0))))
    return out


def _mix(x2d, att, y, u, kvc, w, seq, tm):
    t = x2d.shape[0]
    nseq = seq // tm
    n_mem = kvc.shape[1]
    const = lambda i: (0, 0)
    row = lambda i: (i, 0)
    mem_k = lambda i: (i // nseq, 0, 0)
    mem_v = lambda i: (i // nseq, 0, 1)
    vec = lambda n: pl.BlockSpec((1, n), const)
    return pl.pallas_call(
        _mix_kernel,
        grid=(t // tm,),
        in_specs=[pl.BlockSpec((tm, D_MODEL), row),
                  pl.BlockSpec((tm, ATT_W), row),
                  pl.BlockSpec((tm, SSM_W), row),
                  pl.BlockSpec((tm, SSM_W), row),
                  vec(SSM_W),
                  pl.BlockSpec((SSM_W, SSM_W), const),
                  vec(ATT_W), vec(SSM_W),
                  pl.BlockSpec((ATT_W, D_MODEL), const),
                  pl.BlockSpec((SSM_W, D_MODEL), const),
                  vec(D_MODEL), vec(D_MODEL),
                  pl.BlockSpec((D_MODEL, D_MODEL), const),
                  pl.BlockSpec((1, n_mem, D_MODEL), mem_k),
                  pl.BlockSpec((1, n_mem, D_MODEL), mem_v),
                  pl.BlockSpec((D_MODEL, D_MODEL), const),
                  vec(D_MODEL), vec(D_MODEL),
                  pl.BlockSpec((D_MODEL, ROUTE_W), const),
                  pl.BlockSpec((D_MODEL, ROUTE_W), const),
                  vec(ROUTE_W)],
        out_specs=[pl.BlockSpec((tm * ROW_SUB, LANE), row),
                   pl.BlockSpec((tm, ROUTE_W), row)],
        out_shape=[jax.ShapeDtypeStruct((t * ROW_SUB, LANE), F32),
                   jax.ShapeDtypeStruct((t, ROUTE_W), F32)],
        compiler_params=_cparams(("parallel",)),
        name="mix",
    )(x2d, att, y, u, w["ssm_d"], w["w_glu"], w["g_attn"], w["g_ssm"], w["w_out_a"], w["w_out_s"],
      w["ln1_g"], w["ln1_b"], w["w_cq"], kvc, kvc, w["w_co"], w["ln2_g"], w["ln2_b"],
      w["wr_hi"], w["wr_lo"], w["b_r"])


def _row_copy(src_hbm, dst_vmem, sem, src_row, dst_row):
    return pltpu.make_async_copy(src_hbm.at[pl.ds(pl.multiple_of(src_row * ROW_SUB, ROW_SUB), ROW_SUB)],
                                 dst_vmem.at[pl.ds(pl.multiple_of(dst_row * ROW_SUB, ROW_SUB), ROW_SUB)], sem)


def _load_rows(ref, n_rows):
    return jnp.concatenate([ref[pl.ds(j, n_rows, stride=ROW_SUB), :] for j in range(ROW_SUB)], axis=1)


def _store_rows(ref, val):
    n_rows = val.shape[0]
    for j in range(ROW_SUB):
        ref[pl.ds(j, n_rows, stride=ROW_SUB), :] = val[:, j * LANE:(j + 1) * LANE]


def _gather_rows(src_hbm, idx_smem, dst_vmem, sem, n_rows):
    def issue(r, c):
        _row_copy(src_hbm, dst_vmem, sem, idx_smem[r], r).start()
        return c

    lax.fori_loop(0, n_rows, issue, 0, unroll=GATHER_UNROLL)


def _wait_rows(src_hbm, dst_vmem, sem, n_rows):
    def drain(r, c):
        _row_copy(src_hbm, dst_vmem, sem, 0, r).wait()
        return c

    lax.fori_loop(0, n_rows, drain, 0, unroll=GATHER_UNROLL)


def _fetch_tile(i, idx_hbm, idx_smem, isem, src_hbm, buf, gsem, n_rows):
    cp = pltpu.make_async_copy(idx_hbm.at[i], idx_smem, isem)
    cp.start()
    cp.wait()
    _gather_rows(src_hbm, idx_smem, buf, gsem, n_rows)
    _wait_rows(src_hbm, buf, gsem, n_rows)


def _experts_kernel(te_ref, nv_ref, src_hbm, x_hbm, gate_ref, w1_ref, w3_ref, w2_ref, y_ref,
                    idx_smem, xbuf, isem, gsem):
    i = pl.program_id(0)
    n = nv_ref[0]
    tm = xbuf.shape[0] // ROW_SUB

    @pl.when(i < n)
    def _():
        _fetch_tile(i, src_hbm, idx_smem, isem, x_hbm, xbuf, gsem, tm)
        xb = _load_rows(xbuf, tm).astype(BF16)
        hdn = jax.nn.silu(_dot(xb, w1_ref[0])) * _dot(xb, w3_ref[0])
        _store_rows(y_ref, gate_ref[...] * _dot(hdn.astype(BF16), w2_ref[0]))

    @pl.when(i >= n)
    def _():
        y_ref[...] = jnp.zeros(y_ref.shape, y_ref.dtype)


def _experts(x2, row_src, row_gate, tile_expert, n_valid, w, tm):
    n_tiles = row_src.shape[0]
    ew = lambda i, te, nv: (te[i], 0, 0)
    grid_spec = pltpu.PrefetchScalarGridSpec(
        num_scalar_prefetch=2,
        grid=(n_tiles,),
        in_specs=[pl.BlockSpec(memory_space=pl.ANY),
                  pl.BlockSpec(memory_space=pl.ANY),
                  pl.BlockSpec((tm, 1), lambda i, te, nv: (i, 0)),
                  pl.BlockSpec((1, D_MODEL, D_FF_EXPERT), ew),
                  pl.BlockSpec((1, D_MODEL, D_FF_EXPERT), ew),
                  pl.BlockSpec((1, D_FF_EXPERT, D_MODEL), ew)],
        out_specs=pl.BlockSpec((tm * ROW_SUB, LANE), lambda i, te, nv: (i, 0)),
        scratch_shapes=[pltpu.SMEM((tm,), jnp.int32),
                        pltpu.VMEM((tm * ROW_SUB, LANE), F32),
                        pltpu.SemaphoreType.DMA,
                        pltpu.SemaphoreType.DMA],
    )
    return pl.pallas_call(
        _experts_kernel,
        grid_spec=grid_spec,
        out_shape=jax.ShapeDtypeStruct((n_tiles * tm * ROW_SUB, LANE), F32),
        compiler_params=_cparams(("arbitrary",)),
        name="experts",
    )(tile_expert, n_valid, row_src, x2, row_gate, w["w_e1"], w["w_e3"], w["w_e2"])


def _combine_kernel(pos_hbm, x2_ref, y_hbm, g_ref, b_ref, o_ref, idx_smem, ybuf, isem, gsem):
    i = pl.program_id(0)
    tm = o_ref.shape[0]
    _fetch_tile(i, pos_hbm, idx_smem, isem, y_hbm, ybuf, gsem, TOP_K * tm)
    rows = _load_rows(ybuf, TOP_K * tm)
    moe = rows[0:tm] + rows[tm:2 * tm]
    o_ref[...] = _ln(ALPHA * _load_rows(x2_ref, tm) + moe, g_ref[...], b_ref[...])


def _combine(x2, y_sorted, pos_tiles, w, tm):
    t = x2.shape[0] // ROW_SUB
    const = lambda i: (0, 0)
    row = lambda i: (i, 0)
    return pl.pallas_call(
        _combine_kernel,
        grid=(t // tm,),
        in_specs=[pl.BlockSpec(memory_space=pl.ANY),
                  pl.BlockSpec((tm * ROW_SUB, LANE), row),
                  pl.BlockSpec(memory_space=pl.ANY),
                  pl.BlockSpec((1, D_MODEL), const),
                  pl.BlockSpec((1, D_MODEL), const)],
        out_specs=pl.BlockSpec((tm, D_MODEL), row),
        out_shape=jax.ShapeDtypeStruct((t, D_MODEL), F32),
        scratch_shapes=[pltpu.SMEM((TOP_K * tm,), jnp.int32),
                        pltpu.VMEM((TOP_K * tm * ROW_SUB, LANE), F32),
                        pltpu.SemaphoreType.DMA,
                        pltpu.SemaphoreType.DMA],
        compiler_params=_cparams(("arbitrary",)),
        name="combine",
    )(pos_tiles, x2, y_sorted, w["ln3_g"], w["ln3_b"])


def _lookup(table, idx):
    ids = jnp.arange(table.shape[0], dtype=jnp.int32)[:, None]
    return jnp.sum(jnp.where(idx[None, :] == ids, table[:, None], 0), axis=0)


def _route(gates, tm_e, tm_c):
    t = gates.shape[0]
    flat_e = gates[:, :TOP_K].astype(jnp.int32).reshape(-1)
    flat_w = gates[:, TOP_K:2 * TOP_K].reshape(-1)
    n = flat_e.shape[0]
    ids = jnp.arange(n, dtype=jnp.int32)
    sorted_e, order = lax.sort((flat_e, ids), num_keys=1, is_stable=True)
    _, inv = lax.sort((order, ids), num_keys=1)
    start_unp = jnp.searchsorted(sorted_e, jnp.arange(N_EXPERTS + 1, dtype=jnp.int32), side="left").astype(jnp.int32)
    counts = start_unp[1:] - start_unp[:-1]
    start_unp = start_unp[:-1]
    padded = ((counts + tm_e - 1) // tm_e) * tm_e
    start_pad = jnp.cumsum(padded) - padded
    pos = (inv + _lookup(start_pad - start_unp, flat_e)).astype(jnp.int32)

    n_tiles = n // tm_e + N_EXPERTS
    end_pad = start_pad + padded
    tile_start = jnp.arange(n_tiles, dtype=jnp.int32) * tm_e
    tile_expert = jnp.minimum(jnp.sum(end_pad[:, None] <= tile_start[None, :], axis=0), N_EXPERTS - 1).astype(jnp.int32)
    tile_within = tile_start - _lookup(start_pad, tile_expert)
    r_in = jnp.arange(tm_e, dtype=jnp.int32)[None, :]
    within = tile_within[:, None] + r_in
    valid = within < _lookup(counts, tile_expert)[:, None]
    src = order[jnp.clip(_lookup(start_unp, tile_expert)[:, None] + within, 0, n - 1)]
    row_src = jnp.where(valid, src // TOP_K, 0).astype(jnp.int32)
    row_gate = jnp.where(valid, flat_w[src], 0.0).astype(F32)
    n_valid = (jnp.sum(padded) // tm_e).astype(jnp.int32).reshape(1)
    pos_tiles = pos.reshape(t // tm_c, tm_c, TOP_K).transpose(0, 2, 1).reshape(t // tm_c, TOP_K * tm_c)
    return row_src, row_gate.reshape(n_tiles * tm_e, 1), tile_expert, n_valid, pos_tiles


def _rope_tables(seq):
    inv = ROPE_THETA ** (-jnp.arange(0, ROPE_DIM, 2, dtype=F32) / ROPE_DIM)
    ang = jnp.arange(seq, dtype=F32)[:, None] * inv[None, :]
    c, s = jnp.cos(ang), jnp.sin(ang)
    z = jnp.zeros((seq, HEAD_PAD - NOPE_DIM - ROPE_DIM), F32)
    cos_t = jnp.concatenate([jnp.ones((seq, NOPE_DIM), F32), c, c, z], axis=1)
    sin_t = jnp.concatenate([jnp.zeros((seq, NOPE_DIM), F32), s, s, z], axis=1)
    return cos_t, sin_t


def _pair_rot(w_rope):
    half = ROPE_DIM // 2
    return jnp.concatenate([-w_rope[..., half:], w_rope[..., :half]], axis=-1)


def _ssm_operators(a_re, a_im, log_dt, b_re, b_im, c_re, c_im, max_steps):
    a = lax.complex(a_re, a_im)
    dt = jnp.exp(log_dt)[..., None]
    adt = a * dt
    a_bar = jnp.exp(adt)
    b_bar = ((a_bar - 1.0) / a)[..., None] * lax.complex(b_re, b_im)
    cm = lax.complex(c_re, c_im)
    taus = jnp.arange(SSM_L + 1, dtype=F32)
    pw = jnp.exp(adt[None] * taus[:, None, None, None])
    kf = jnp.real(jnp.einsum("gcp,tgp,gpd->tgcd", cm[0], pw[:SSM_L, 0], b_bar[0]))
    kb = jnp.real(jnp.einsum("gcp,tgp,gpd->tgcd", cm[1], pw[:SSM_L, 1], b_bar[1]))
    kfull = jnp.concatenate([kb[:0:-1], kf[:1] + kb[:1], kf[1:]], axis=0)
    ii = jnp.arange(SSM_L)
    lag = ii[:, None] - ii[None, :] + SSM_L - 1
    mt = kfull[lag]
    mt = mt.transpose(2, 1, 4, 0, 3)
    pf = pw[SSM_L - 1 - ii, 0]
    pb = pw[ii, 1]
    sf = pf[..., None] * b_bar[0][None]
    sb = pb[..., None] * b_bar[1][None]
    bin_ = jnp.stack([jnp.real(sf), jnp.imag(sf), jnp.real(sb), jnp.imag(sb)], axis=2)
    bin_ = bin_.transpose(1, 0, 4, 2, 3)
    of = cm[0][None] * pw[1 + ii, 0][:, :, None, :]
    ob = cm[1][None] * pw[SSM_L - ii, 1][:, :, None, :]
    cout = jnp.stack([jnp.real(of), -jnp.imag(of), jnp.real(ob), -jnp.imag(ob)], axis=3)
    cout = cout.transpose(1, 3, 4, 0, 2)
    n_blk = SSM_G // SSM_LG
    cw = SSM_L * LANE
    sw4 = 4 * SSM_LG * SSM_P

    def place(outer, inner):
        o, g, k = jnp.meshgrid(jnp.arange(outer), jnp.arange(SSM_LG), jnp.arange(inner), indexing="ij")
        src_col = (o * inner + k).reshape(-1)
        grp = g.reshape(-1)
        onehot = (jnp.arange(outer * inner)[None, :, None] == src_col[None, None, :]) & \
                 (jnp.arange(SSM_LG)[:, None, None] == grp[None, None, :])
        return onehot.astype(BF16)

    def expand(t, outer, inner):
        t = t.astype(BF16).reshape((n_blk, SSM_LG) + t.shape[1:])
        return jnp.einsum("ng...x,gxy->n...gy", t, place(outer, inner), preferred_element_type=BF16)

    mt_b = expand(mt.reshape(SSM_G, SSM_L, SSM_GC, SSM_L * SSM_GC), SSM_L, SSM_GC)
    mt_b = mt_b.transpose(0, 1, 3, 2, 4).reshape(n_blk, cw, cw)
    bin_b = expand(bin_.reshape(SSM_G, SSM_L, SSM_GC, 4 * SSM_P), 4, SSM_P)
    bin_b = bin_b.transpose(0, 1, 3, 2, 4).reshape(n_blk, cw, sw4)
    cout_b = expand(cout.reshape(SSM_G, 4, SSM_P, SSM_L * SSM_GC), SSM_L, SSM_GC)
    cout_b = cout_b.transpose(0, 1, 3, 2, 4).reshape(n_blk, sw4, cw)
    steps = (SSM_L * (2.0 ** jnp.arange(max_steps, dtype=F32)))
    sp = jnp.exp(adt[None] * steps[:, None, None, None])
    tab = jnp.stack([jnp.real(sp[:, 0]), jnp.imag(sp[:, 0]), jnp.real(sp[:, 1]), jnp.imag(sp[:, 1])],
                    axis=1)
    tab = tab.reshape(max_steps, 4, n_blk, SSM_LG * SSM_P).transpose(2, 0, 1, 3)
    return mt_b, bin_b, cout_b, tab.astype(F32)


def _prepare(p, max_chunks):
    w = {}
    w_in = p["w_in"]
    o_kpe = Q_LORA + KV_LORA
    kpe = w_in[:, o_kpe:o_kpe + ROPE_DIM]
    kpe_blk = jnp.concatenate([jnp.zeros((D_MODEL, NOPE_DIM), F32), kpe, _pair_rot(kpe)], axis=1)
    w["w_in"] = jnp.concatenate([w_in[:, :o_kpe], w_in[:, o_kpe + ROPE_DIM:], kpe_blk], axis=1).astype(BF16)
    w["g_q"] = p["g_q_lat"].reshape(1, -1)
    w["g_kv"] = p["g_kv_lat"].reshape(1, -1)
    uq = p["w_uq"].reshape(Q_LORA, MLA_HEADS, QK_DIM)
    rope = uq[..., NOPE_DIM:]
    w["w_uq"] = jnp.concatenate([uq, _pair_rot(rope)], axis=-1).reshape(Q_LORA, -1).astype(BF16)
    ukv = p["w_ukv"].reshape(KV_LORA, MLA_HEADS, NOPE_DIM + V_DIM)
    zpad = jnp.zeros((KV_LORA, MLA_HEADS, HEAD_PAD - NOPE_DIM), F32)
    w["w_kn"] = jnp.concatenate([ukv[..., :NOPE_DIM], zpad], axis=-1).reshape(KV_LORA, -1).astype(BF16)
    w["w_v"] = jnp.concatenate([ukv[..., NOPE_DIM:], zpad], axis=-1).reshape(KV_LORA, -1).astype(BF16)
    one = jnp.zeros((MLA_HEADS, HEAD_PAD), F32).at[:, V_DIM].set(1.0)
    w["v_one"] = one.reshape(1, -1)
    max_steps = max(1, int(math.log2(max_chunks)))
    w["ssm_mt"], w["ssm_bin"], w["ssm_cout"], w["ssm_pw"] = _ssm_operators(
        p["ssm_a_re"], p["ssm_a_im"], p["ssm_log_dt"], p["ssm_b_re"], p["ssm_b_im"],
        p["ssm_c_re"], p["ssm_c_im"], max_steps)
    w["ssm_d"] = p["ssm_d"].reshape(1, -1)
    w["w_glu"] = p["w_glu"].astype(BF16)
    w["g_attn"] = p["g_attn_grp"].reshape(1, -1)
    w["g_ssm"] = p["g_ssm_grp"].reshape(1, -1)
    w["w_out_a"] = p["w_out"][:ATT_W].astype(BF16)
    w["w_out_s"] = p["w_out"][ATT_W:].astype(BF16)
    for n in ("ln1_g", "ln1_b", "ln2_g", "ln2_b", "ln3_g", "ln3_b"):
        w[n] = p[n].reshape(1, -1)
    w["w_cq"] = p["w_cq"].astype(BF16)
    w["w_ckv"] = p["w_ckv"].astype(BF16)
    w["w_co"] = p["w_co"].astype(BF16)
    wr = jnp.concatenate([p["w_rg"], p["w_re"],
                          jnp.zeros((D_MODEL, ROUTE_W - N_GROUPS - N_EXPERTS), F32)], axis=1)
    w["wr_hi"], w["wr_lo"] = _split_bf16(wr)
    w["b_r"] = jnp.concatenate([p["b_rg"], p["b_re"],
                                jnp.zeros((ROUTE_W - N_GROUPS - N_EXPERTS,), F32)]).reshape(1, -1)
    w["w_e1"] = p["w_e1"].astype(BF16)
    w["w_e3"] = p["w_e3"].astype(BF16)
    w["w_e2"] = p["w_e2"].astype(BF16)
    return w


def _tile(n, pref):
    return pref if n % pref == 0 else n


def _run(x, mem, w):
    b, s, _ = x.shape
    t = b * s
    n_mem = mem.shape[1]
    x2d = x.reshape(t, D_MODEL)
    tm = _tile(s, TOKEN_TILE)
    scale = QK_DIM ** -0.5 * math.log2(math.e)
    cos_t, sin_t = _rope_tables(s)
    wl = dict(w, cos_q=cos_t * scale, sin_q=sin_t * scale, cos_k=cos_t, sin_k=sin_t)

    kvc = _matmul(mem.reshape(b * n_mem, D_MODEL), w["w_ckv"], BF16, n_mem, D_MODEL)
    kvc = kvc.reshape(b, n_mem, 2 * D_MODEL)

    q, k, v, u = _in_proj(x2d, wl, s, tm)
    hw = MLA_HEADS * HEAD_PAD
    att = _mla(q.reshape(b, s, hw), k.reshape(b, s, hw), v.reshape(b, s, hw), tm, s)

    y = _s5(u.reshape(b, s, SSM_W), w).reshape(t, SSM_W)

    x2, gates = _mix(x2d, att.reshape(t, ATT_W), y, u, kvc, w, s, tm)

    tm_e = EXPERT_TILE
    tm_c = _tile(t, TOKEN_TILE)
    row_src, row_gate, tile_expert, n_valid, pos_tiles = _route(gates, tm_e, tm_c)
    y_sorted = _experts(x2, row_src, row_gate, tile_expert, n_valid, w, tm_e)
    out = _combine(x2, y_sorted, pos_tiles, w, tm_c)
    return out.reshape(b, s, D_MODEL)


def kernel(x_prompt, x_sample, mem_prompt, mem_sample, w_in, g_q_lat, w_uq, g_kv_lat, w_ukv, ssm_a_re, ssm_a_im, ssm_log_dt, ssm_b_re, ssm_b_im, ssm_c_re, ssm_c_im, ssm_d, w_glu, g_attn_grp, g_ssm_grp, w_out, ln1_g, ln1_b, w_cq, w_ckv, w_co, ln2_g, ln2_b, w_rg, b_rg, w_re, b_re, w_e1, w_e3, w_e2, ln3_g, ln3_b):
    params = dict(w_in=w_in, g_q_lat=g_q_lat, w_uq=w_uq, g_kv_lat=g_kv_lat, w_ukv=w_ukv,
                  ssm_a_re=ssm_a_re, ssm_a_im=ssm_a_im, ssm_log_dt=ssm_log_dt, ssm_b_re=ssm_b_re,
                  ssm_b_im=ssm_b_im, ssm_c_re=ssm_c_re, ssm_c_im=ssm_c_im, ssm_d=ssm_d, w_glu=w_glu,
                  g_attn_grp=g_attn_grp, g_ssm_grp=g_ssm_grp, w_out=w_out, ln1_g=ln1_g, ln1_b=ln1_b,
                  w_cq=w_cq, w_ckv=w_ckv, w_co=w_co, ln2_g=ln2_g, ln2_b=ln2_b,
                  w_rg=w_rg, b_rg=b_rg, w_re=w_re, b_re=b_re, w_e1=w_e1, w_e3=w_e3, w_e2=w_e2,
                  ln3_g=ln3_g, ln3_b=ln3_b)
    params = {name: val[0] for name, val in params.items()}
    max_chunks = max(x_prompt.shape[1], x_sample.shape[1]) // SSM_L
    w = _prepare(params, max_chunks)
    return (_run(x_prompt, mem_prompt, w), _run(x_sample, mem_sample, w))
```

```python
import functools
import math

import jax
import jax.numpy as jnp
from jax import lax
from jax.experimental import pallas as pl
from jax.experimental.pallas import tpu as pltpu
from jax.experimental.pallas import tpu_sc as plsc

F32 = jnp.float32
BF16 = jnp.bfloat16

D_MODEL = 1024
MLA_HEADS = 8
NOPE_DIM = 64
ROPE_DIM = 32
V_DIM = 64
QK_DIM = NOPE_DIM + ROPE_DIM
Q_LORA = 384
KV_LORA = 256
ATT_W = MLA_HEADS * V_DIM
ROPE_THETA = 10000.0
SSM_W = D_MODEL - ATT_W
SSM_GC = 16
SSM_G = SSM_W // SSM_GC
SSM_P = 64
XA_HEADS = 4
XA_DIM = D_MODEL // XA_HEADS
N_GROUPS = 4
EXP_PER_GROUP = 8
N_EXPERTS = N_GROUPS * EXP_PER_GROUP
TOP_K = 2
D_FF_EXPERT = 256
DEPTH = 1
ALPHA = (2 * DEPTH) ** 0.25
LN_EPS = 1e-5
RMS_EPS = 1e-6

LANE = 128
HEAD_PAD = 128
SSM_L = 16
SSM_LG = LANE // SSM_GC
ROUTE_W = 128
ROW_SUB = D_MODEL // LANE
SC_WINDOW = 128
TOKEN_TILE = 512
EXPERT_TILE = 512
VMEM_LIMIT = 56 * 1024 * 1024


def _cparams(sem):
    return pltpu.CompilerParams(dimension_semantics=sem, vmem_limit_bytes=VMEM_LIMIT)


def _rms(x, g):
    return x * lax.rsqrt(jnp.mean(x * x, axis=-1, keepdims=True) + RMS_EPS) * g


def _ln(x, g, b):
    mu = jnp.mean(x, axis=-1, keepdims=True)
    xc = x - mu
    var = jnp.mean(xc * xc, axis=-1, keepdims=True)
    return xc * lax.rsqrt(var + LN_EPS) * g + b


def _dot(a, b):
    return jnp.dot(a, b, preferred_element_type=F32)


def _dot_t(a, b):
    return lax.dot_general(a, b, (((1,), (1,)), ((), ())), preferred_element_type=F32)


def _matmul_kernel(x_ref, w_ref, o_ref):
    o_ref[...] = _dot(x_ref[...].astype(BF16), w_ref[...]).astype(o_ref.dtype)


def _matmul(x, w, out_dtype, tm, tn):
    m, k = x.shape
    n = w.shape[1]
    return pl.pallas_call(
        _matmul_kernel,
        grid=(m // tm, n // tn),
        in_specs=[pl.BlockSpec((tm, k), lambda i, j: (i, 0)),
                  pl.BlockSpec((k, tn), lambda i, j: (0, j))],
        out_specs=pl.BlockSpec((tm, tn), lambda i, j: (i, j)),
        out_shape=jax.ShapeDtypeStruct((m, n), out_dtype),
        compiler_params=_cparams(("parallel", "parallel")),
        name="mem_kv",
    )(x, w)


def _rot(blk, cos, sin):
    return blk * cos + pltpu.roll(blk, HEAD_PAD - ROPE_DIM, axis=1) * sin


def _in_proj_kernel(x_ref, w_in_ref, gq_ref, gkv_ref, w_uq_ref, w_kn_ref, w_v_ref, vone_ref,
                    cq_ref, sq_ref, ck_ref, sk_ref,
                    q_ref, k_ref, v_ref, u_ref):
    xb = x_ref[...].astype(BF16)
    proj = _dot(xb, w_in_ref[...])
    o_kv = Q_LORA
    o_u = Q_LORA + KV_LORA
    o_kr = o_u + SSM_W
    u_ref[...] = proj[:, o_u:o_kr]
    qn = _rms(proj[:, :Q_LORA], gq_ref[...]).astype(BF16)
    kvn = _rms(proj[:, o_kv:o_u], gkv_ref[...]).astype(BF16)
    qa = _dot(qn, w_uq_ref[...])
    kn = _dot(kvn, w_kn_ref[...])
    v = _dot(kvn, w_v_ref[...]) + vone_ref[...]
    v_ref[...] = v.astype(BF16)
    cq, sq, ck, sk = cq_ref[...], sq_ref[...], ck_ref[...], sk_ref[...]
    kr = _rot(proj[:, o_kr:o_kr + HEAD_PAD], ck, sk)
    for h in range(MLA_HEADS):
        sl = slice(h * HEAD_PAD, (h + 1) * HEAD_PAD)
        q_ref[:, sl] = _rot(qa[:, sl], cq, sq).astype(BF16)
        k_ref[:, sl] = (kn[:, sl] + kr).astype(BF16)


def _in_proj(x2d, w, seq, tm):
    t = x2d.shape[0]
    nseq = seq // tm
    hw = MLA_HEADS * HEAD_PAD
    const = lambda i: (0, 0)
    row = lambda i: (i, 0)
    pos = lambda i: (i % nseq, 0)
    n_in = w["w_in"].shape[1]
    return pl.pallas_call(
        _in_proj_kernel,
        grid=(t // tm,),
        in_specs=[pl.BlockSpec((tm, D_MODEL), row),
                  pl.BlockSpec((D_MODEL, n_in), const),
                  pl.BlockSpec((1, Q_LORA), const),
                  pl.BlockSpec((1, KV_LORA), const),
                  pl.BlockSpec((Q_LORA, hw), const),
                  pl.BlockSpec((KV_LORA, hw), const),
                  pl.BlockSpec((KV_LORA, hw), const),
                  pl.BlockSpec((1, hw), const),
                  pl.BlockSpec((tm, HEAD_PAD), pos),
                  pl.BlockSpec((tm, HEAD_PAD), pos),
                  pl.BlockSpec((tm, HEAD_PAD), pos),
                  pl.BlockSpec((tm, HEAD_PAD), pos)],
        out_specs=[pl.BlockSpec((tm, hw), row),
                   pl.BlockSpec((tm, hw), row),
                   pl.BlockSpec((tm, hw), row),
                   pl.BlockSpec((tm, SSM_W), row)],
        out_shape=[jax.ShapeDtypeStruct((t, hw), BF16),
                   jax.ShapeDtypeStruct((t, hw), BF16),
                   jax.ShapeDtypeStruct((t, hw), BF16),
                   jax.ShapeDtypeStruct((t, SSM_W), F32)],
        compiler_params=_cparams(("parallel",)),
        name="in_proj",
    )(x2d, w["w_in"], w["g_q"], w["g_kv"], w["w_uq"], w["w_kn"], w["w_v"], w["v_one"],
      w["cos_q"], w["sin_q"], w["cos_k"], w["sin_k"])


def _shift_rows(x, d, down):
    n = x.shape[0]
    r = lax.broadcasted_iota(jnp.int32, x.shape, 0)
    if down:
        return jnp.where(r >= d, pltpu.roll(x, d, axis=0), 0.0)
    return jnp.where(r < n - d, pltpu.roll(x, n - d, axis=0), 0.0)


def _s5_kernel(u_ref, mt_ref, bin_ref, cout_ref, pw_ref, y_ref, *, n_steps):
    nc = u_ref.shape[1] // SSM_L
    sw = SSM_LG * SSM_P
    lhs = jnp.concatenate([u_ref[0, pl.ds(s, nc, stride=SSM_L), :].astype(BF16) for s in range(SSM_L)],
                          axis=1)
    hin = _dot(lhs, bin_ref[0])
    f_re, f_im, b_re, b_im = (hin[:, q * sw:(q + 1) * sw] for q in range(4))
    for s in range(n_steps):
        d = 1 << s
        ar, ai = pw_ref[0, s, 0:1, :], pw_ref[0, s, 1:2, :]
        sr, si = _shift_rows(f_re, d, True), _shift_rows(f_im, d, True)
        f_re, f_im = f_re + ar * sr - ai * si, f_im + ar * si + ai * sr
        ar, ai = pw_ref[0, s, 2:3, :], pw_ref[0, s, 3:4, :]
        sr, si = _shift_rows(b_re, d, False), _shift_rows(b_im, d, False)
        b_re, b_im = b_re + ar * sr - ai * si, b_im + ar * si + ai * sr
    hprev = jnp.concatenate([_shift_rows(f_re, 1, True), _shift_rows(f_im, 1, True),
                             _shift_rows(b_re, 1, False), _shift_rows(b_im, 1, False)], axis=1)
    y = _dot(lhs, mt_ref[0]) + _dot(hprev.astype(BF16), cout_ref[0])
    for i in range(SSM_L):
        y_ref[0, pl.ds(i, nc, stride=SSM_L), :] = y[:, i * LANE:(i + 1) * LANE]


def _s5(u, w):
    b, s, _ = u.shape
    nc = s // SSM_L
    n_steps = max(1, int(math.log2(nc)))
    assert (1 << n_steps) == nc
    pw = w["ssm_pw"][:, :n_steps]
    n_blk = SSM_W // LANE
    cw = SSM_L * LANE
    once = pl.Buffered(1)
    return pl.pallas_call(
        functools.partial(_s5_kernel, n_steps=n_steps),
        grid=(n_blk, b),
        in_specs=[pl.BlockSpec((1, s, LANE), lambda gi, bi: (bi, 0, gi)),
                  pl.BlockSpec((1, cw, cw), lambda gi, bi: (gi, 0, 0), pipeline_mode=once),
                  pl.BlockSpec((1, cw, 4 * SSM_LG * SSM_P), lambda gi, bi: (gi, 0, 0), pipeline_mode=once),
                  pl.BlockSpec((1, 4 * SSM_LG * SSM_P, cw), lambda gi, bi: (gi, 0, 0), pipeline_mode=once),
                  pl.BlockSpec((1, n_steps, 4, SSM_LG * SSM_P), lambda gi, bi: (gi, 0, 0, 0))],
        out_specs=pl.BlockSpec((1, s, LANE), lambda gi, bi: (bi, 0, gi)),
        out_shape=jax.ShapeDtypeStruct((b, s, SSM_W), F32),
        compiler_params=_cparams(("arbitrary", "arbitrary")),
        name="s5",
    )(u, w["ssm_mt"], w["ssm_bin"], w["ssm_cout"], pw)


def _mla_kernel(q_ref, k_ref, v_ref, o_ref, *, tk):
    s_len = k_ref.shape[1]
    tq = q_ref.shape[1]
    qs = [q_ref[0, :, hh * HEAD_PAD:(hh + 1) * HEAD_PAD] for hh in range(2)]

    def body(j, carry):
        off = pl.multiple_of(j * tk, tk)
        new = []
        for hh in range(2):
            m, acc = carry[hh]
            sl = slice(hh * HEAD_PAD, (hh + 1) * HEAD_PAD)
            s = _dot_t(qs[hh], k_ref[0, pl.ds(off, tk), sl])
            m_new = jnp.maximum(m, jnp.max(s, axis=-1, keepdims=True))
            p = jnp.exp2(s - m_new)
            acc = jnp.exp2(m - m_new) * acc + _dot(p.astype(BF16), v_ref[0, pl.ds(off, tk), sl])
            new.append((m_new, acc))
        return tuple(new)

    init = tuple((jnp.full((tq, 1), -jnp.inf, F32), jnp.zeros((tq, HEAD_PAD), F32)) for _ in range(2))
    res = lax.fori_loop(0, s_len // tk, body, init, unroll=True)
    outs = [acc / acc[:, V_DIM:V_DIM + 1] for _, acc in res]
    lane = lax.broadcasted_iota(jnp.int32, (tq, HEAD_PAD), 1)
    o = jnp.where(lane < V_DIM, outs[0], pltpu.roll(outs[1], V_DIM, axis=1))
    o_ref[0] = o.astype(o_ref.dtype)


def _mla(q, k, v, tq, tk):
    b, s, hw = q.shape
    return pl.pallas_call(
        functools.partial(_mla_kernel, tk=tk),
        grid=(b, MLA_HEADS // 2, s // tq),
        in_specs=[pl.BlockSpec((1, tq, 2 * HEAD_PAD), lambda bi, hi, qi: (bi, qi, hi)),
                  pl.BlockSpec((1, s, 2 * HEAD_PAD), lambda bi, hi, qi: (bi, 0, hi)),
                  pl.BlockSpec((1, s, 2 * HEAD_PAD), lambda bi, hi, qi: (bi, 0, hi))],
        out_specs=pl.BlockSpec((1, tq, 2 * V_DIM), lambda bi, hi, qi: (bi, qi, hi)),
        out_shape=jax.ShapeDtypeStruct((b, s, ATT_W), BF16),
        compiler_params=_cparams(("parallel", "parallel", "arbitrary")),
        name="mla",
    )(q, k, v)


def _split_bf16(x):
    hi = x.astype(BF16)
    lo = (x - hi.astype(F32)).astype(BF16)
    return hi, lo


def _mix_kernel(x_ref, att_ref, y_ref, u_ref, d_ref, w_glu_ref, ga_ref, gs_ref, wo_a_ref, wo_s_ref,
                ln1g_ref, ln1b_ref, w_cq_ref, kc_ref, vc_ref, w_co_ref, ln2g_ref, ln2b_ref,
                wr_hi_ref, wr_lo_ref, br_ref, x2_ref, lg_ref):
    y = y_ref[...] + d_ref[...] * u_ref[...]
    gy = jax.nn.gelu(y)
    ssm = gy * jax.nn.sigmoid(_dot(gy.astype(BF16), w_glu_ref[...]))
    att = att_ref[...].astype(F32)
    mixed = (_dot(_rms(att, ga_ref[...]).astype(BF16), wo_a_ref[...])
             + _dot(_rms(ssm, gs_ref[...]).astype(BF16), wo_s_ref[...]))
    x1 = _ln(ALPHA * x_ref[...] + mixed, ln1g_ref[...], ln1b_ref[...])
    qc = (_dot(x1.astype(BF16), w_cq_ref[...]) * (XA_DIM ** -0.5)).astype(BF16)
    heads = []
    for h in range(XA_HEADS):
        sl = slice(h * XA_DIM, (h + 1) * XA_DIM)
        s = _dot_t(qc[:, sl], kc_ref[0, :, sl])
        p = jnp.exp(s - jnp.max(s, axis=-1, keepdims=True))
        o = _dot(p.astype(BF16), vc_ref[0, :, sl])
        heads.append(o / jnp.sum(p, axis=-1, keepdims=True))
    ca = _dot(jnp.concatenate(heads, axis=1).astype(BF16), w_co_ref[...])
    x2 = _ln(ALPHA * x1 + ca, ln2g_ref[...], ln2b_ref[...])
    _store_rows(x2_ref, x2)
    hi, lo = _split_bf16(x2)
    logits = (_dot(hi, wr_hi_ref[...]) + _dot(lo, wr_hi_ref[...]) + _dot(hi, wr_lo_ref[...])
              + br_ref[...])
    lg_ref[...] = _gates(logits)


def _first_max(x, lane):
    v = jnp.max(x, axis=-1, keepdims=True)
    i = jnp.min(jnp.where(x == v, lane, ROUTE_W), axis=-1, keepdims=True)
    return v, i


def _gates(logits):
    lane = lax.broadcasted_iota(jnp.int32, logits.shape, 1)
    is_grp = lane < N_GROUPS
    lg = jnp.where(is_grp, logits, -jnp.inf)
    g_max, grp = _first_max(lg, lane)
    p_grp = 1.0 / jnp.sum(jnp.where(is_grp, jnp.exp(lg - g_max), 0.0), axis=-1, keepdims=True)
    e_lane = lane - N_GROUPS
    in_grp = (e_lane >= 0) & (e_lane < N_EXPERTS) & ((e_lane // EXP_PER_GROUP) == grp)
    le = jnp.where(in_grp, logits, -jnp.inf)
    v1, i1 = _first_max(le, lane)
    v2, i2 = _first_max(jnp.where(lane == i1, -jnp.inf, le), lane)
    e21 = jnp.exp(v2 - v1)
    w1 = p_grp / (1.0 + e21)
    w2 = w1 * e21
    out = jnp.where(lane == 0, (i1 - N_GROUPS).astype(F32),
                    jnp.where(lane == 1, (i2 - N_GROUPS).astype(F32),
                              jnp.where(lane == 2, w1, jnp.where(lane == 3, w2, 0.0))))
    return out


def _mix(x2d, att, y, u, kvc, w, seq, tm):
    t = x2d.shape[0]
    nseq = seq // tm
    n_mem = kvc.shape[1]
    const = lambda i: (0, 0)
    row = lambda i: (i, 0)
    mem_k = lambda i: (i // nseq, 0, 0)
    mem_v = lambda i: (i // nseq, 0, 1)
    vec = lambda n: pl.BlockSpec((1, n), const)
    return pl.pallas_call(
        _mix_kernel,
        grid=(t // tm,),
        in_specs=[pl.BlockSpec((tm, D_MODEL), row),
                  pl.BlockSpec((tm, ATT_W), row),
                  pl.BlockSpec((tm, SSM_W), row),
                  pl.BlockSpec((tm, SSM_W), row),
                  vec(SSM_W),
                  pl.BlockSpec((SSM_W, SSM_W), const),
                  vec(ATT_W), vec(SSM_W),
                  pl.BlockSpec((ATT_W, D_MODEL), const),
                  pl.BlockSpec((SSM_W, D_MODEL), const),
                  vec(D_MODEL), vec(D_MODEL),
                  pl.BlockSpec((D_MODEL, D_MODEL), const),
                  pl.BlockSpec((1, n_mem, D_MODEL), mem_k),
                  pl.BlockSpec((1, n_mem, D_MODEL), mem_v),
                  pl.BlockSpec((D_MODEL, D_MODEL), const),
                  vec(D_MODEL), vec(D_MODEL),
                  pl.BlockSpec((D_MODEL, ROUTE_W), const),
                  pl.BlockSpec((D_MODEL, ROUTE_W), const),
                  vec(ROUTE_W)],
        out_specs=[pl.BlockSpec((tm * ROW_SUB, LANE), row),
                   pl.BlockSpec((tm, ROUTE_W), row)],
        out_shape=[jax.ShapeDtypeStruct((t * ROW_SUB, LANE), F32),
                   jax.ShapeDtypeStruct((t, ROUTE_W), F32)],
        compiler_params=_cparams(("parallel",)),
        name="mix",
    )(x2d, att, y, u, w["ssm_d"], w["w_glu"], w["g_attn"], w["g_ssm"], w["w_out_a"], w["w_out_s"],
      w["ln1_g"], w["ln1_b"], w["w_cq"], kvc, kvc, w["w_co"], w["ln2_g"], w["ln2_b"],
      w["wr_hi"], w["wr_lo"], w["b_r"])


def _load_rows(ref, n_rows):
    return jnp.concatenate([ref[pl.ds(j, n_rows, stride=ROW_SUB), :] for j in range(ROW_SUB)], axis=1)


def _store_rows(ref, val):
    n_rows = val.shape[0]
    for j in range(ROW_SUB):
        ref[pl.ds(j, n_rows, stride=ROW_SUB), :] = val[:, j * LANE:(j + 1) * LANE]


def _sc_gather(x, idx):
    n = idx.shape[0]
    mesh = plsc.VectorSubcoreMesh(core_axis_name="c", subcore_axis_name="s")

    @pl.kernel(out_type=jax.ShapeDtypeStruct((n, LANE), x.dtype), mesh=mesh, scratch_types=[])
    def gather(x_hbm, i_hbm, o_hbm):
        def body(i_vmem, o_vmem):
            pltpu.sync_copy(x_hbm.at[i_vmem.at[0]], o_vmem)

        pltpu.emit_pipeline(
            body,
            grid=(n // SC_WINDOW,),
            in_specs=[pl.BlockSpec((1, SC_WINDOW), lambda i: (0, i))],
            out_specs=[pl.BlockSpec((SC_WINDOW, LANE), lambda i: (i, 0))],
            core_axis_name=("c", "s"),
            dimension_semantics=(pltpu.PARALLEL,),
        )(i_hbm, o_hbm)

    return gather(x, idx.reshape(1, n))


def _gather_tokens(x_tiles, rows):
    sub = jnp.arange(ROW_SUB, dtype=jnp.int32)[None, :]
    return _sc_gather(x_tiles, (rows.reshape(-1, 1) * ROW_SUB + sub).reshape(-1))


def _experts_kernel(te_ref, nv_ref, x_ref, gate_ref, w1_ref, w3_ref, w2_ref, y_ref):
    i = pl.program_id(0)
    n = nv_ref[0]
    tm = x_ref.shape[0] // ROW_SUB

    @pl.when(i < n)
    def _():
        xb = _load_rows(x_ref, tm).astype(BF16)
        hdn = jax.nn.silu(_dot(xb, w1_ref[0])) * _dot(xb, w3_ref[0])
        _store_rows(y_ref, gate_ref[...] * _dot(hdn.astype(BF16), w2_ref[0]))

    @pl.when(i >= n)
    def _():
        y_ref[...] = jnp.zeros(y_ref.shape, y_ref.dtype)


def _experts(x_sorted, row_gate, tile_expert, n_valid, w, tm):
    n_tiles = tile_expert.shape[0]
    ew = lambda i, te, nv: (te[i], 0, 0)
    rows = lambda i, te, nv: (i, 0)
    grid_spec = pltpu.PrefetchScalarGridSpec(
        num_scalar_prefetch=2,
        grid=(n_tiles,),
        in_specs=[pl.BlockSpec((tm * ROW_SUB, LANE), rows),
                  pl.BlockSpec((tm, 1), rows),
                  pl.BlockSpec((1, D_MODEL, D_FF_EXPERT), ew),
                  pl.BlockSpec((1, D_MODEL, D_FF_EXPERT), ew),
                  pl.BlockSpec((1, D_FF_EXPERT, D_MODEL), ew)],
        out_specs=pl.BlockSpec((tm * ROW_SUB, LANE), rows),
    )
    return pl.pallas_call(
        _experts_kernel,
        grid_spec=grid_spec,
        out_shape=jax.ShapeDtypeStruct((n_tiles * tm * ROW_SUB, LANE), F32),
        compiler_params=_cparams(("arbitrary",)),
        name="experts",
    )(tile_expert, n_valid, x_sorted, row_gate, w["w_e1"], w["w_e3"], w["w_e2"])


def _combine_kernel(x2_ref, ya_ref, yb_ref, g_ref, b_ref, o_ref):
    tm = o_ref.shape[0]
    moe = _load_rows(ya_ref, tm) + _load_rows(yb_ref, tm)
    o_ref[...] = _ln(ALPHA * _load_rows(x2_ref, tm) + moe, g_ref[...], b_ref[...])


def _combine(x2, y_pair, w, tm):
    t = x2.shape[0] // ROW_SUB
    nt = t // tm
    const = lambda i: (0, 0)
    row = lambda i: (i, 0)
    return pl.pallas_call(
        _combine_kernel,
        grid=(nt,),
        in_specs=[pl.BlockSpec((tm * ROW_SUB, LANE), row),
                  pl.BlockSpec((tm * ROW_SUB, LANE), row),
                  pl.BlockSpec((tm * ROW_SUB, LANE), lambda i: (i + nt, 0)),
                  pl.BlockSpec((1, D_MODEL), const),
                  pl.BlockSpec((1, D_MODEL), const)],
        out_specs=pl.BlockSpec((tm, D_MODEL), row),
        out_shape=jax.ShapeDtypeStruct((t, D_MODEL), F32),
        compiler_params=_cparams(("parallel",)),
        name="combine",
    )(x2, y_pair, y_pair, w["ln3_g"], w["ln3_b"])


def _lookup(table, idx):
    ids = jnp.arange(table.shape[0], dtype=jnp.int32)[:, None]
    return jnp.sum(jnp.where(idx[None, :] == ids, table[:, None], 0), axis=0)


def _route(gates, tm_e):
    t = gates.shape[0]
    flat_e = gates[:, :TOP_K].astype(jnp.int32).reshape(-1)
    flat_w = gates[:, TOP_K:2 * TOP_K].reshape(-1)
    n = flat_e.shape[0]
    ids = jnp.arange(n, dtype=jnp.int32)
    sorted_e, order = lax.sort((flat_e, ids), num_keys=1, is_stable=True)
    _, inv = lax.sort((order, ids), num_keys=1)
    start_unp = jnp.searchsorted(sorted_e, jnp.arange(N_EXPERTS + 1, dtype=jnp.int32), side="left").astype(jnp.int32)
    counts = start_unp[1:] - start_unp[:-1]
    start_unp = start_unp[:-1]
    padded = ((counts + tm_e - 1) // tm_e) * tm_e
    start_pad = jnp.cumsum(padded) - padded
    pos = (inv + _lookup(start_pad - start_unp, flat_e)).astype(jnp.int32)

    n_tiles = n // tm_e + N_EXPERTS
    end_pad = start_pad + padded
    tile_start = jnp.arange(n_tiles, dtype=jnp.int32) * tm_e
    tile_expert = jnp.minimum(jnp.sum(end_pad[:, None] <= tile_start[None, :], axis=0), N_EXPERTS - 1).astype(jnp.int32)
    tile_within = tile_start - _lookup(start_pad, tile_expert)
    r_in = jnp.arange(tm_e, dtype=jnp.int32)[None, :]
    within = tile_within[:, None] + r_in
    valid = within < _lookup(counts, tile_expert)[:, None]
    src = order[jnp.clip(_lookup(start_unp, tile_expert)[:, None] + within, 0, n - 1)]
    row_src = jnp.where(valid, src // TOP_K, 0).astype(jnp.int32)
    row_gate = jnp.where(valid, flat_w[src], 0.0).astype(F32)
    n_valid = (jnp.sum(padded) // tm_e).astype(jnp.int32).reshape(1)
    pos_k = pos.reshape(t, TOP_K).T.reshape(-1)
    return row_src.reshape(-1), row_gate.reshape(n_tiles * tm_e, 1), tile_expert, n_valid, pos_k


def _rope_tables(seq):
    inv = ROPE_THETA ** (-jnp.arange(0, ROPE_DIM, 2, dtype=F32) / ROPE_DIM)
    ang = jnp.arange(seq, dtype=F32)[:, None] * inv[None, :]
    c, s = jnp.cos(ang), jnp.sin(ang)
    z = jnp.zeros((seq, HEAD_PAD - NOPE_DIM - ROPE_DIM), F32)
    cos_t = jnp.concatenate([jnp.ones((seq, NOPE_DIM), F32), c, c, z], axis=1)
    sin_t = jnp.concatenate([jnp.zeros((seq, NOPE_DIM), F32), s, s, z], axis=1)
    return cos_t, sin_t


def _pair_rot(w_rope):
    half = ROPE_DIM // 2
    return jnp.concatenate([-w_rope[..., half:], w_rope[..., :half]], axis=-1)


def _ssm_operators(a_re, a_im, log_dt, b_re, b_im, c_re, c_im, max_steps):
    a = lax.complex(a_re, a_im)
    dt = jnp.exp(log_dt)[..., None]
    adt = a * dt
    a_bar = jnp.exp(adt)
    b_bar = ((a_bar - 1.0) / a)[..., None] * lax.complex(b_re, b_im)
    cm = lax.complex(c_re, c_im)
    taus = jnp.arange(SSM_L + 1, dtype=F32)
    pw = jnp.exp(adt[None] * taus[:, None, None, None])
    kf = jnp.real(jnp.einsum("gcp,tgp,gpd->tgcd", cm[0], pw[:SSM_L, 0], b_bar[0]))
    kb = jnp.real(jnp.einsum("gcp,tgp,gpd->tgcd", cm[1], pw[:SSM_L, 1], b_bar[1]))
    kfull = jnp.concatenate([kb[:0:-1], kf[:1] + kb[:1], kf[1:]], axis=0)
    ii = jnp.arange(SSM_L)
    lag = ii[:, None] - ii[None, :] + SSM_L - 1
    mt = kfull[lag]
    mt = mt.transpose(2, 1, 4, 0, 3)
    pf = pw[SSM_L - 1 - ii, 0]
    pb = pw[ii, 1]
    sf = pf[..., None] * b_bar[0][None]
    sb = pb[..., None] * b_bar[1][None]
    bin_ = jnp.stack([jnp.real(sf), jnp.imag(sf), jnp.real(sb), jnp.imag(sb)], axis=2)
    bin_ = bin_.transpose(1, 0, 4, 2, 3)
    of = cm[0][None] * pw[1 + ii, 0][:, :, None, :]
    ob = cm[1][None] * pw[SSM_L - ii, 1][:, :, None, :]
    cout = jnp.stack([jnp.real(of), -jnp.imag(of), jnp.real(ob), -jnp.imag(ob)], axis=3)
    cout = cout.transpose(1, 3, 4, 0, 2)
    n_blk = SSM_G // SSM_LG
    cw = SSM_L * LANE
    sw4 = 4 * SSM_LG * SSM_P

    def place(outer, inner):
        o, g, k = jnp.meshgrid(jnp.arange(outer), jnp.arange(SSM_LG), jnp.arange(inner), indexing="ij")
        src_col = (o * inner + k).reshape(-1)
        grp = g.reshape(-1)
        onehot = (jnp.arange(outer * inner)[None, :, None] == src_col[None, None, :]) & \
                 (jnp.arange(SSM_LG)[:, None, None] == grp[None, None, :])
        return onehot.astype(BF16)

    def expand(t, outer, inner):
        t = t.astype(BF16).reshape((n_blk, SSM_LG) + t.shape[1:])
        return jnp.einsum("ng...x,gxy->n...gy", t, place(outer, inner), preferred_element_type=BF16)

    mt_b = expand(mt.reshape(SSM_G, SSM_L, SSM_GC, SSM_L * SSM_GC), SSM_L, SSM_GC)
    mt_b = mt_b.transpose(0, 1, 3, 2, 4).reshape(n_blk, cw, cw)
    bin_b = expand(bin_.reshape(SSM_G, SSM_L, SSM_GC, 4 * SSM_P), 4, SSM_P)
    bin_b = bin_b.transpose(0, 1, 3, 2, 4).reshape(n_blk, cw, sw4)
    cout_b = expand(cout.reshape(SSM_G, 4, SSM_P, SSM_L * SSM_GC), SSM_L, SSM_GC)
    cout_b = cout_b.transpose(0, 1, 3, 2, 4).reshape(n_blk, sw4, cw)
    steps = (SSM_L * (2.0 ** jnp.arange(max_steps, dtype=F32)))
    sp = jnp.exp(adt[None] * steps[:, None, None, None])
    tab = jnp.stack([jnp.real(sp[:, 0]), jnp.imag(sp[:, 0]), jnp.real(sp[:, 1]), jnp.imag(sp[:, 1])],
                    axis=1)
    tab = tab.reshape(max_steps, 4, n_blk, SSM_LG * SSM_P).transpose(2, 0, 1, 3)
    return mt_b, bin_b, cout_b, tab.astype(F32)


def _prepare(p, max_chunks):
    w = {}
    w_in = p["w_in"]
    o_kpe = Q_LORA + KV_LORA
    kpe = w_in[:, o_kpe:o_kpe + ROPE_DIM]
    kpe_blk = jnp.concatenate([jnp.zeros((D_MODEL, NOPE_DIM), F32), kpe, _pair_rot(kpe)], axis=1)
    w["w_in"] = jnp.concatenate([w_in[:, :o_kpe], w_in[:, o_kpe + ROPE_DIM:], kpe_blk], axis=1).astype(BF16)
    w["g_q"] = p["g_q_lat"].reshape(1, -1)
    w["g_kv"] = p["g_kv_lat"].reshape(1, -1)
    uq = p["w_uq"].reshape(Q_LORA, MLA_HEADS, QK_DIM)
    rope = uq[..., NOPE_DIM:]
    w["w_uq"] = jnp.concatenate([uq, _pair_rot(rope)], axis=-1).reshape(Q_LORA, -1).astype(BF16)
    ukv = p["w_ukv"].reshape(KV_LORA, MLA_HEADS, NOPE_DIM + V_DIM)
    zpad = jnp.zeros((KV_LORA, MLA_HEADS, HEAD_PAD - NOPE_DIM), F32)
    w["w_kn"] = jnp.concatenate([ukv[..., :NOPE_DIM], zpad], axis=-1).reshape(KV_LORA, -1).astype(BF16)
    w["w_v"] = jnp.concatenate([ukv[..., NOPE_DIM:], zpad], axis=-1).reshape(KV_LORA, -1).astype(BF16)
    one = jnp.zeros((MLA_HEADS, HEAD_PAD), F32).at[:, V_DIM].set(1.0)
    w["v_one"] = one.reshape(1, -1)
    max_steps = max(1, int(math.log2(max_chunks)))
    w["ssm_mt"], w["ssm_bin"], w["ssm_cout"], w["ssm_pw"] = _ssm_operators(
        p["ssm_a_re"], p["ssm_a_im"], p["ssm_log_dt"], p["ssm_b_re"], p["ssm_b_im"],
        p["ssm_c_re"], p["ssm_c_im"], max_steps)
    w["ssm_d"] = p["ssm_d"].reshape(1, -1)
    w["w_glu"] = p["w_glu"].astype(BF16)
    w["g_attn"] = p["g_attn_grp"].reshape(1, -1)
    w["g_ssm"] = p["g_ssm_grp"].reshape(1, -1)
    w["w_out_a"] = p["w_out"][:ATT_W].astype(BF16)
    w["w_out_s"] = p["w_out"][ATT_W:].astype(BF16)
    for n in ("ln1_g", "ln1_b", "ln2_g", "ln2_b", "ln3_g", "ln3_b"):
        w[n] = p[n].reshape(1, -1)
    w["w_cq"] = p["w_cq"].astype(BF16)
    w["w_ckv"] = p["w_ckv"].astype(BF16)
    w["w_co"] = p["w_co"].astype(BF16)
    wr = jnp.concatenate([p["w_rg"], p["w_re"],
                          jnp.zeros((D_MODEL, ROUTE_W - N_GROUPS - N_EXPERTS), F32)], axis=1)
    w["wr_hi"], w["wr_lo"] = _split_bf16(wr)
    w["b_r"] = jnp.concatenate([p["b_rg"], p["b_re"],
                                jnp.zeros((ROUTE_W - N_GROUPS - N_EXPERTS,), F32)]).reshape(1, -1)
    w["w_e1"] = p["w_e1"].astype(BF16)
    w["w_e3"] = p["w_e3"].astype(BF16)
    w["w_e2"] = p["w_e2"].astype(BF16)
    return w


def _tile(n, pref):
    return pref if n % pref == 0 else n


def _run(x, mem, w):
    b, s, _ = x.shape
    t = b * s
    n_mem = mem.shape[1]
    x2d = x.reshape(t, D_MODEL)
    tm = _tile(s, TOKEN_TILE)
    scale = QK_DIM ** -0.5 * math.log2(math.e)
    cos_t, sin_t = _rope_tables(s)
    wl = dict(w, cos_q=cos_t * scale, sin_q=sin_t * scale, cos_k=cos_t, sin_k=sin_t)

    kvc = _matmul(mem.reshape(b * n_mem, D_MODEL), w["w_ckv"], BF16, n_mem, D_MODEL)
    kvc = kvc.reshape(b, n_mem, 2 * D_MODEL)

    q, k, v, u = _in_proj(x2d, wl, s, tm)
    hw = MLA_HEADS * HEAD_PAD
    att = _mla(q.reshape(b, s, hw), k.reshape(b, s, hw), v.reshape(b, s, hw), tm, s)

    y = _s5(u.reshape(b, s, SSM_W), w).reshape(t, SSM_W)

    x2, gates = _mix(x2d, att.reshape(t, ATT_W), y, u, kvc, w, s, tm)

    tm_e = EXPERT_TILE
    tm_c = _tile(t, TOKEN_TILE)
    row_src, row_gate, tile_expert, n_valid, pos_k = _route(gates, tm_e)
    y_sorted = _experts(_gather_tokens(x2, row_src), row_gate, tile_expert, n_valid, w, tm_e)
    out = _combine(x2, _gather_tokens(y_sorted, pos_k), w, tm_c)
    return out.reshape(b, s, D_MODEL)


def kernel(x_prompt, x_sample, mem_prompt, mem_sample, w_in, g_q_lat, w_uq, g_kv_lat, w_ukv, ssm_a_re, ssm_a_im, ssm_log_dt, ssm_b_re, ssm_b_im, ssm_c_re, ssm_c_im, ssm_d, w_glu, g_attn_grp, g_ssm_grp, w_out, ln1_g, ln1_b, w_cq, w_ckv, w_co, ln2_g, ln2_b, w_rg, b_rg, w_re, b_re, w_e1, w_e3, w_e2, ln3_g, ln3_b):
    params = dict(w_in=w_in, g_q_lat=g_q_lat, w_uq=w_uq, g_kv_lat=g_kv_lat, w_ukv=w_ukv,
                  ssm_a_re=ssm_a_re, ssm_a_im=ssm_a_im, ssm_log_dt=ssm_log_dt, ssm_b_re=ssm_b_re,
                  ssm_b_im=ssm_b_im, ssm_c_re=ssm_c_re, ssm_c_im=ssm_c_im, ssm_d=ssm_d, w_glu=w_glu,
                  g_attn_grp=g_attn_grp, g_ssm_grp=g_ssm_grp, w_out=w_out, ln1_g=ln1_g, ln1_b=ln1_b,
                  w_cq=w_cq, w_ckv=w_ckv, w_co=w_co, ln2_g=ln2_g, ln2_b=ln2_b,
                  w_rg=w_rg, b_rg=b_rg, w_re=w_re, b_re=b_re, w_e1=w_e1, w_e3=w_e3, w_e2=w_e2,
                  ln3_g=ln3_g, ln3_b=ln3_b)
    params = {name: val[0] for name, val in params.items()}
    max_chunks = max(x_prompt.shape[1], x_sample.shape[1]) // SSM_L
    w = _prepare(params, max_chunks)
    return (_run(x_prompt, mem_prompt, w), _run(x_sample, mem_sample, w))
```
